```python
import math
import jax, jax.numpy as jnp
from jax import lax
import numpy as np

D_MODEL = 1024
BATCH = 16
SEQ = 2048
DEPTH = 2

CHUNK = 64
Q_BLOCK = 128
ATT_HEADS = 8
ATT_HEAD_DIM = 64
ATT_WIDTH = ATT_HEADS * ATT_HEAD_DIM
IDX_HEADS = 8
IDX_DIM = 32
TOPK_MAX = 256
TOPK_FRACTION = 4
SSM_WIDTH = D_MODEL // 4
SSM_GROUP = 16
SSM_GROUPS = SSM_WIDTH // SSM_GROUP
SSM_STATE = 64
DT_MIN = 1e-3
DT_MAX = 1e-1
CONV_WIDTH = D_MODEL // 4
CONV_KSIZE = 31
MIX_WIDTH = ATT_WIDTH + SSM_WIDTH + CONV_WIDTH
D_FF = 2816
N_MOD = 9
NORM_EPS = 1e-6
IN_COLS = (ATT_WIDTH, ATT_HEAD_DIM, ATT_HEAD_DIM, IDX_HEADS * IDX_DIM, IDX_DIM, IDX_HEADS, SSM_WIDTH, 2 * CONV_WIDTH)
IN_WIDTH = sum(IN_COLS)
IN_SPLITS = tuple(int(s) for s in np.cumsum(IN_COLS)[:-1])

kernel_name = 'chunk_causal_hybrid_dsa_s5_conformer'


def rms_norm(x, g):
    xf = x.astype(jnp.float32)
    y = xf * lax.rsqrt(jnp.mean(xf * xf, axis=-1, keepdims=True) + NORM_EPS)
    return (y * g.astype(jnp.float32)).astype(x.dtype)


def layer_norm(x, g, b):
    xf = x.astype(jnp.float32)
    mu = jnp.mean(xf, axis=-1, keepdims=True)
    var = jnp.mean(jnp.square(xf - mu), axis=-1, keepdims=True)
    y = (xf - mu) * lax.rsqrt(var + NORM_EPS) * g.astype(jnp.float32) + b.astype(jnp.float32)
    return y.astype(x.dtype)


def modulate(x, shift, scale):
    return x * (1.0 + scale) + shift


def swiglu_ffn(h, w_gu, w_down):
    gate, up = jnp.split(h @ w_gu, 2, axis=-1)
    return (jax.nn.silu(gate) * up) @ w_down


def dsa_attention(q, k, v, qi, ki, wi):
    bsz, seq_len = q.shape[0], q.shape[1]
    topk = min(TOPK_MAX, seq_len // TOPK_FRACTION)
    n_blocks = seq_len // Q_BLOCK
    key_chunk = jnp.arange(seq_len) // CHUNK
    ki_f = ki.astype(jnp.float32)

    def to_blocks(t):
        return jnp.moveaxis(t.reshape((bsz, n_blocks, Q_BLOCK) + t.shape[2:]), 1, 0)

    def one_block(args):
        qb, qib, wib, blk = args
        t_chunk = (blk * Q_BLOCK + jnp.arange(Q_BLOCK)) // CHUNK
        admissible = key_chunk[None, :] <= t_chunk[:, None]
        s = jnp.einsum('bqhd,bsd->bqhs', qib.astype(jnp.float32), ki_f) * (IDX_DIM ** -0.5)
        w_h = wib.astype(jnp.float32) * (IDX_HEADS ** -0.5)
        idx_score = jnp.einsum('bqhs,bqh->bqs', jax.nn.relu(s), w_h)
        idx_score = jnp.where(admissible[None], idx_score, -jnp.inf)
        _, sel = lax.top_k(idx_score, topk)
        k_sel = jax.vmap(lambda kk, ii: kk[ii])(k, sel)
        v_sel = jax.vmap(lambda vv, ii: vv[ii])(v, sel)
        valid = key_chunk[sel] <= t_chunk[None, :, None]
        logits = jnp.einsum('bqhd,bqkd->bqhk', qb, k_sel).astype(jnp.float32) * (ATT_HEAD_DIM ** -0.5)
        logits = jnp.where(valid[:, :, None, :], logits, -jnp.inf)
        p = jax.nn.softmax(logits, axis=-1).astype(v.dtype)
        o = jnp.einsum('bqhk,bqkd->bqhd', p, v_sel)
        return o.reshape(bsz, Q_BLOCK, ATT_WIDTH)

    out = lax.map(one_block, (to_blocks(q), to_blocks(qi), to_blocks(wi), jnp.arange(n_blocks)))
    return jnp.moveaxis(out, 0, 1).reshape(bsz, seq_len, ATT_WIDTH)


def _cmul(ar, ai, br, bi):
    return ar * br - ai * bi, ar * bi + ai * br


def _ssm_combine(e1, e2):
    a1r, a1i, b1r, b1i = e1
    a2r, a2i, b2r, b2i = e2
    ar, ai = _cmul(a2r, a2i, a1r, a1i)
    br, bi = _cmul(a2r, a2i, b1r, b1i)
    return ar, ai, br + b2r, bi + b2i


def s5_mixer(u, a_re, a_im, log_dt, b_re, b_im, c_re, c_im, d_skip, w_glu):
    f32 = jnp.float32
    bsz, seq_len, _ = u.shape
    uf = u.astype(f32).reshape(bsz, seq_len, SSM_GROUPS, SSM_GROUP)
    dt = jnp.exp(log_dt.astype(f32))[:, None]
    ar, ai = a_re.astype(f32), a_im.astype(f32)
    mag = jnp.exp(dt * ar)
    abr, abi = mag * jnp.cos(dt * ai), mag * jnp.sin(dt * ai)
    den = ar * ar + ai * ai
    nr, ni = abr - 1.0, abi
    qr, qi = (nr * ar + ni * ai) / den, (ni * ar - nr * ai) / den
    bbr, bbi = _cmul(qr[..., None], qi[..., None], b_re.astype(f32), b_im.astype(f32))
    bu_re = jnp.einsum('blgc,gpc->blgp', uf, bbr)
    bu_im = jnp.einsum('blgc,gpc->blgp', uf, bbi)
    abr_l = jnp.broadcast_to(abr, (seq_len,) + abr.shape)
    abi_l = jnp.broadcast_to(abi, (seq_len,) + abi.shape)

    def scan_one(br, bi):
        _, _, xr, xi = lax.associative_scan(_ssm_combine, (abr_l, abi_l, br, bi), axis=0)
        return xr, xi

    xr, xi = jax.vmap(scan_one)(bu_re, bu_im)
    y = (jnp.einsum('blgp,gcp->blgc', xr, c_re.astype(f32))
         - jnp.einsum('blgp,gcp->blgc', xi, c_im.astype(f32))
         + d_skip.astype(f32) * uf)
    y = jax.nn.gelu(y).reshape(bsz, seq_len, SSM_WIDTH).astype(u.dtype)
    a, g = jnp.split(y @ w_glu, 2, axis=-1)
    return a * jax.nn.sigmoid(g)


def conformer_conv(h, w_dw, b_dw, ln_g, ln_b, w_pw):
    a, g = jnp.split(h, 2, axis=-1)
    u = a * jax.nn.sigmoid(g)
    y = lax.conv_general_dilated(u, w_dw[:, None, :], window_strides=(1,),
                                 padding=((CONV_KSIZE - 1, 0),),
                                 dimension_numbers=('NWC', 'WIO', 'NWC'),
                                 feature_group_count=CONV_WIDTH) + b_dw
    y = jax.nn.silu(layer_norm(y, ln_g, ln_b))
    return y @ w_pw


def hybrid_mixer(h, w_in, w_out, att_out_norm, ssm_out_norm, conv_out_norm,
                 a_re, a_im, log_dt, b_re, b_im, c_re, c_im, d_skip, w_glu,
                 w_dw, b_dw, ln_g, ln_b, w_pw):
    bsz, seq_len, _ = h.shape
    proj = h @ w_in
    q, k, v, qi, ki, wi, u_ssm, u_conv = jnp.split(proj, IN_SPLITS, axis=-1)
    y_att = dsa_attention(q.reshape(bsz, seq_len, ATT_HEADS, ATT_HEAD_DIM), k, v,
                          qi.reshape(bsz, seq_len, IDX_HEADS, IDX_DIM), ki, wi)
    y_ssm = s5_mixer(u_ssm, a_re, a_im, log_dt, b_re, b_im, c_re, c_im, d_skip, w_glu)
    y_conv = conformer_conv(u_conv, w_dw, b_dw, ln_g, ln_b, w_pw)
    merged = jnp.concatenate([rms_norm(y_att, att_out_norm),
                              rms_norm(y_ssm, ssm_out_norm),
                              rms_norm(y_conv, conv_out_norm)], axis=-1)
    return merged @ w_out


def setup_inputs(seed: int = 0) -> dict:
    key = jax.random.key(seed)
    ks = iter(jax.random.split(key, 40))
    f32 = jnp.float32

    def nrm(shape, scale):
        return jax.random.normal(next(ks), shape, f32) * scale

    def gain(shape):
        return 1.0 + nrm(shape, 0.02)

    n_idx = jnp.arange(SSM_STATE, dtype=f32)
    return {
        'x': nrm((BATCH, SEQ, D_MODEL), 1.0),
        'c': nrm((BATCH, D_MODEL), 1.0),
        'mod_w': nrm((DEPTH, D_MODEL, N_MOD * D_MODEL), 0.5 * D_MODEL ** -0.5),
        'mod_b': nrm((DEPTH, N_MOD * D_MODEL), 0.01),
        'ffn1_norm': gain((DEPTH, D_MODEL)),
        'ffn1_w_gu': nrm((DEPTH, D_MODEL, 2 * D_FF), D_MODEL ** -0.5),
        'ffn1_w_down': nrm((DEPTH, D_FF, D_MODEL), D_FF ** -0.5),
        'mix_norm': gain((DEPTH, D_MODEL)),
        'w_in': nrm((DEPTH, D_MODEL, IN_WIDTH), D_MODEL ** -0.5),
        'w_out': nrm((DEPTH, MIX_WIDTH, D_MODEL), MIX_WIDTH ** -0.5),
        'att_out_norm': gain((DEPTH, ATT_WIDTH)),
        'ssm_out_norm': gain((DEPTH, SSM_WIDTH)),
        'conv_out_norm': gain((DEPTH, CONV_WIDTH)),
        'ssm_a_re': -0.5 + nrm((DEPTH, SSM_GROUPS, SSM_STATE), 0.01),
        'ssm_a_im': jnp.pi * n_idx + nrm((DEPTH, SSM_GROUPS, SSM_STATE), 0.01),
        'ssm_log_dt': jax.random.uniform(next(ks), (DEPTH, SSM_GROUPS), f32,
                                         math.log(DT_MIN), math.log(DT_MAX)),
        'ssm_b_re': nrm((DEPTH, SSM_GROUPS, SSM_STATE, SSM_GROUP), (2 * SSM_GROUP) ** -0.5),
        'ssm_b_im': nrm((DEPTH, SSM_GROUPS, SSM_STATE, SSM_GROUP), (2 * SSM_GROUP) ** -0.5),
        'ssm_c_re': nrm((DEPTH, SSM_GROUPS, SSM_GROUP, SSM_STATE), (2 * SSM_STATE) ** -0.5),
        'ssm_c_im': nrm((DEPTH, SSM_GROUPS, SSM_GROUP, SSM_STATE), (2 * SSM_STATE) ** -0.5),
        'ssm_d': nrm((DEPTH, SSM_GROUPS, SSM_GROUP), 1.0),
        'ssm_w_glu': nrm((DEPTH, SSM_WIDTH, 2 * SSM_WIDTH), SSM_WIDTH ** -0.5),
        'conv_w_dw': nrm((DEPTH, CONV_KSIZE, CONV_WIDTH), CONV_KSIZE ** -0.5),
        'conv_b_dw': nrm((DEPTH, CONV_WIDTH), 0.01),
        'conv_ln_g': gain((DEPTH, CONV_WIDTH)),
        'conv_ln_b': nrm((DEPTH, CONV_WIDTH), 0.01),
        'conv_w_pw': nrm((DEPTH, CONV_WIDTH, CONV_WIDTH), CONV_WIDTH ** -0.5),
        'ffn2_norm': gain((DEPTH, D_MODEL)),
        'ffn2_w_gu': nrm((DEPTH, D_MODEL, 2 * D_FF), D_MODEL ** -0.5),
        'ffn2_w_down': nrm((DEPTH, D_FF, D_MODEL), D_FF ** -0.5),
        'final_norm': gain((D_MODEL,)),
    }


def reference(x, c, mod_w, mod_b, ffn1_norm, ffn1_w_gu, ffn1_w_down, mix_norm, w_in, w_out,
              att_out_norm, ssm_out_norm, conv_out_norm, ssm_a_re, ssm_a_im, ssm_log_dt,
              ssm_b_re, ssm_b_im, ssm_c_re, ssm_c_im, ssm_d, ssm_w_glu, conv_w_dw, conv_b_dw,
              conv_ln_g, conv_ln_b, conv_w_pw, ffn2_norm, ffn2_w_gu, ffn2_w_down, final_norm):
    cond = jax.nn.silu(c)[:, None, :]
    for l in range(DEPTH):
        mod = cond @ mod_w[l] + mod_b[l]
        sh1, sc1, g1, sh2, sc2, g2, sh3, sc3, g3 = jnp.split(mod, N_MOD, axis=-1)
        h = modulate(rms_norm(x, ffn1_norm[l]), sh1, sc1)
        x = x + 0.5 * g1 * swiglu_ffn(h, ffn1_w_gu[l], ffn1_w_down[l])
        h = modulate(rms_norm(x, mix_norm[l]), sh2, sc2)
        x = x + g2 * hybrid_mixer(h, w_in[l], w_out[l], att_out_norm[l], ssm_out_norm[l],
                                  conv_out_norm[l], ssm_a_re[l], ssm_a_im[l], ssm_log_dt[l],
                                  ssm_b_re[l], ssm_b_im[l], ssm_c_re[l], ssm_c_im[l], ssm_d[l],
                                  ssm_w_glu[l], conv_w_dw[l], conv_b_dw[l], conv_ln_g[l],
                                  conv_ln_b[l], conv_w_pw[l])
        h = modulate(rms_norm(x, ffn2_norm[l]), sh3, sc3)
        x = x + 0.5 * g3 * swiglu_ffn(h, ffn2_w_gu[l], ffn2_w_down[l])
    return rms_norm(x, final_norm)
```

```python
import functools
import math

import jax
import jax.numpy as jnp
from jax import lax
from jax.experimental import pallas as pl
from jax.experimental.pallas import tpu as pltpu

F32 = jnp.float32
BF16 = jnp.bfloat16

D_MODEL = 1024
CHUNK = 64
ATT_HEADS = 8
ATT_HEAD_DIM = 64
ATT_WIDTH = ATT_HEADS * ATT_HEAD_DIM
IDX_HEADS = 8
IDX_DIM = 32
TOPK_MAX = 256
TOPK_FRACTION = 4
SSM_WIDTH = 256
SSM_GROUP = 16
SSM_GROUPS = 16
SSM_STATE = 64
SSM_LANES = 2 * SSM_GROUPS * SSM_STATE
SSM_SUB = 8
CONV_WIDTH = 256
CONV_KSIZE = 31
CONV_HALO = 32
D_FF = 2816
N_MOD = 9
NORM_EPS = 1e-6
IN_COLS = (ATT_WIDTH, ATT_HEAD_DIM, ATT_HEAD_DIM, IDX_HEADS * IDX_DIM, IDX_DIM, IDX_HEADS, SSM_WIDTH, 2 * CONV_WIDTH)

VMEM_LIMIT = 56 * 1024 * 1024
NEG_BIG = -1e30
KEY_NEG_INF = -2139095041
KEY_POS_INF = 2139095040
HI = lax.Precision.HIGHEST


def _cparams(sem):
    return pltpu.CompilerParams(dimension_semantics=sem, vmem_limit_bytes=VMEM_LIMIT)


def _resident(shape):
    nd = len(shape)
    return pl.BlockSpec(shape, lambda *_: (0,) * nd, pipeline_mode=pl.Buffered(1))


def _nt_dot(a, b):
    return lax.dot_general(a, b, (((1,), (1,)), ((), ())), preferred_element_type=F32)


def _modulated_norm(x, gain, shift, scale):
    y = x * lax.rsqrt(jnp.mean(x * x, axis=-1, keepdims=True) + NORM_EPS) * gain
    return y * (1.0 + scale) + shift


def _mod_kernel(c_ref, w_ref, b_ref, o_ref):
    c = c_ref[...]
    cond = c * jax.nn.sigmoid(c)
    o_ref[...] = jnp.dot(cond, w_ref[...], preferred_element_type=F32, precision=HI) + b_ref[...]


def _mod_call(c, mod_w, mod_b):
    depth = mod_w.shape[0]
    bsz = c.shape[0]
    return pl.pallas_call(
        _mod_kernel,
        grid=(depth, N_MOD),
        in_specs=[
            pl.BlockSpec((bsz, D_MODEL), lambda l, j: (0, 0)),
            pl.BlockSpec((None, D_MODEL, D_MODEL), lambda l, j: (l, 0, j)),
            pl.BlockSpec((None, 1, D_MODEL), lambda l, j: (l, 0, j)),
        ],
        out_specs=pl.BlockSpec((None, bsz, D_MODEL), lambda l, j: (l, 0, j)),
        out_shape=jax.ShapeDtypeStruct((depth, bsz, N_MOD * D_MODEL), F32),
        compiler_params=_cparams(("arbitrary", "arbitrary")),
        name="adaln_mod",
    )(c, mod_w, mod_b.reshape(depth, 1, N_MOD * D_MODEL))


FFN_CHUNK = 256


def _ffn_kernel(x_ref, mod_ref, ng_ref, wgu_ref, wd_ref, fn_ref, o_ref, *, row0, final):
    x = x_ref[...]
    h = _modulated_norm(x, ng_ref[...], mod_ref[row0:row0 + 1, :], mod_ref[row0 + 1:row0 + 2, :]).astype(BF16)
    acc = jnp.zeros(x.shape, F32)
    for c0 in range(0, D_FF, FFN_CHUNK):
        gate = jnp.dot(h, wgu_ref[:, c0:c0 + FFN_CHUNK], preferred_element_type=F32)
        up = jnp.dot(h, wgu_ref[:, D_FF + c0:D_FF + c0 + FFN_CHUNK], preferred_element_type=F32)
        a = (gate * jax.nn.sigmoid(gate) * up).astype(BF16)
        acc = acc + jnp.dot(a, wd_ref[c0:c0 + FFN_CHUNK, :], preferred_element_type=F32)
    out = x + (0.5 * mod_ref[row0 + 2:row0 + 3, :]) * acc
    if final:
        out = out * lax.rsqrt(jnp.mean(out * out, axis=-1, keepdims=True) + NORM_EPS) * fn_ref[...]
    o_ref[...] = out


def _ffn_call(x, mod3, norm_g, w_gu, w_down, final_g, *, row0, final, tm):
    bsz, seq, _ = x.shape
    nt = seq // tm
    return pl.pallas_call(
        functools.partial(_ffn_kernel, row0=row0, final=final),
        grid=(bsz, nt),
        in_specs=[
            pl.BlockSpec((None, tm, D_MODEL), lambda b, t: (b, t, 0)),
            pl.BlockSpec((None, N_MOD, D_MODEL), lambda b, t: (b, 0, 0)),
            _resident((1, D_MODEL)),
            _resident((D_MODEL, 2 * D_FF)),
            _resident((D_FF, D_MODEL)),
            _resident((1, D_MODEL)),
        ],
        out_specs=pl.BlockSpec((None, tm, D_MODEL), lambda b, t: (b, t, 0)),
        out_shape=jax.ShapeDtypeStruct(x.shape, F32),
        compiler_params=_cparams(("arbitrary", "arbitrary")),
        name="ffn_final" if final else "ffn",
    )(x, mod3, norm_g, w_gu, w_down, final_g)


PROJ_Q0, PROJ_KV0, PROJ_QI0, PROJ_KW0, PROJ_SSM0, PROJ_CONV0, PROJ_END = 0, 512, 640, 896, 1024, 1280, 1792


def _proj_kernel(x_ref, mod_ref, ng_ref, w_ref, wit_ref, q_ref, k_ref, v_ref, qi_ref, ki_ref, wi_ref, us_ref, uc_ref):
    x = x_ref[...]
    h = _modulated_norm(x, ng_ref[...], mod_ref[3:4, :], mod_ref[4:5, :]).astype(BF16)
    q = jnp.dot(h, w_ref[:, PROJ_Q0:PROJ_KV0], preferred_element_type=F32) * (ATT_HEAD_DIM ** -0.5)
    for hd in range(ATT_HEADS):
        q_ref[hd] = q[:, hd * ATT_HEAD_DIM:(hd + 1) * ATT_HEAD_DIM].astype(BF16)
    kv = jnp.dot(h, w_ref[:, PROJ_KV0:PROJ_QI0], preferred_element_type=F32)
    k_ref[...] = kv[:, :ATT_HEAD_DIM].astype(BF16)
    v_ref[...] = kv[:, ATT_HEAD_DIM:].astype(BF16)
    qi = jnp.dot(h, w_ref[:, PROJ_QI0:PROJ_KW0], preferred_element_type=F32)
    for hd in range(IDX_HEADS):
        qi_ref[hd] = qi[:, hd * IDX_DIM:(hd + 1) * IDX_DIM].astype(BF16)
    kw = jnp.dot(h, w_ref[:, PROJ_KW0:PROJ_SSM0], preferred_element_type=F32)
    ki_ref[...] = kw[:, :IDX_DIM].astype(BF16)
    wi_ref[...] = _nt_dot(wit_ref[...], h) * (IDX_DIM ** -0.5 * IDX_HEADS ** -0.5)
    us_ref[...] = jnp.dot(h, w_ref[:, PROJ_SSM0:PROJ_CONV0], preferred_element_type=F32)
    uc_ref[...] = jnp.dot(h, w_ref[:, PROJ_CONV0:PROJ_END], preferred_element_type=F32)


def _proj_call(x, mod3, norm_g, w_in_r, w_wi_t, *, tm):
    bsz, seq, _ = x.shape
    nt = seq // tm
    tok = lambda width: pl.BlockSpec((None, tm, width), lambda b, t: (b, t, 0))
    head = lambda n, width: pl.BlockSpec((None, n, tm, width), lambda b, t: (b, 0, t, 0))
    return pl.pallas_call(
        _proj_kernel,
        grid=(bsz, nt),
        in_specs=[
            tok(D_MODEL),
            pl.BlockSpec((None, N_MOD, D_MODEL), lambda b, t: (b, 0, 0)),
            _resident((1, D_MODEL)),
            _resident((D_MODEL, PROJ_END)),
            _resident((IDX_HEADS, D_MODEL)),
        ],
        out_specs=[
            head(ATT_HEADS, ATT_HEAD_DIM), tok(ATT_HEAD_DIM), tok(ATT_HEAD_DIM),
            head(IDX_HEADS, IDX_DIM), tok(IDX_DIM),
            pl.BlockSpec((None, IDX_HEADS, tm), lambda b, t: (b, 0, t)),
            tok(SSM_WIDTH), tok(2 * CONV_WIDTH),
        ],
        out_shape=[
            jax.ShapeDtypeStruct((bsz, ATT_HEADS, seq, ATT_HEAD_DIM), BF16),
            jax.ShapeDtypeStruct((bsz, seq, ATT_HEAD_DIM), BF16),
            jax.ShapeDtypeStruct((bsz, seq, ATT_HEAD_DIM), BF16),
            jax.ShapeDtypeStruct((bsz, IDX_HEADS, seq, IDX_DIM), BF16),
            jax.ShapeDtypeStruct((bsz, seq, IDX_DIM), BF16),
            jax.ShapeDtypeStruct((bsz, IDX_HEADS, seq), F32),
            jax.ShapeDtypeStruct((bsz, seq, SSM_WIDTH), F32),
            jax.ShapeDtypeStruct((bsz, seq, 2 * CONV_WIDTH), F32),
        ],
        compiler_params=_cparams(("arbitrary", "arbitrary")),
        name="mixer_in_proj",
    )(x, mod3, norm_g, w_in_r, w_wi_t)


def _key_to_f32(key):
    return lax.bitcast_convert_type(jnp.where(key >= 0, key, key ^ 0x7FFFFFFF), F32)


def _att_kernel(q_ref, k_ref, v_ref, qi_ref, ki_ref, wi_ref, o_ref,
                idx_scr, lg_scr, m_scr, l_scr, acc_scr, j_scr, *, tq, kb, topk, n_tie_iter):
    j = pl.program_id(1)
    nkb = (j * tq + tq + kb - 1) // kb
    heads = ATT_HEADS
    rows = heads * tq
    half = kb // 2

    qcol = lax.broadcasted_iota(jnp.int32, (1, tq), 1)
    limit = j * tq + ((qcol // CHUNK) + 1) * CHUNK
    key0 = lax.broadcasted_iota(jnp.int32, (kb, tq), 0)
    wt = wi_ref[...]

    def idx_body(b, carry):
        kib = ki_ref[pl.ds(pl.multiple_of(b * kb, kb), kb), :]
        acc = jnp.zeros((kb, tq), F32)
        for hd in range(IDX_HEADS):
            acc = acc + jnp.maximum(_nt_dot(kib, qi_ref[hd]), 0.0) * wt[hd:hd + 1, :]
        idx_scr[b] = jnp.where(key0 + b * kb < limit, acc, -jnp.inf)
        return carry

    lax.fori_loop(0, nkb, idx_body, 0)

    def reduce_keys(fn, init, combine):
        def body(b, acc):
            return combine(acc, fn(idx_scr[b], b).reshape(kb // 8, 8, tq))
        return lax.fori_loop(0, nkb, body, jnp.full((8, tq), init, F32))

    def count(pred):
        acc = reduce_keys(lambda blk, b: jnp.where(pred(blk, b), 1.0, 0.0), 0.0, lambda a, m: a + jnp.sum(m, axis=0))
        return jnp.sum(acc, axis=0, keepdims=True)

    def search_body(_, carry):
        lo, hi = carry
        mid = (lo | hi) - ((lo ^ hi) >> 1)
        cand = _key_to_f32(mid)
        ok = count(lambda blk, b: blk >= cand) >= topk
        return jnp.where(ok, mid, lo), jnp.where(ok, hi, mid - 1)

    lo, _ = lax.fori_loop(0, 32, search_body,
                          (jnp.full((1, tq), KEY_NEG_INF, jnp.int32), jnp.full((1, tq), KEY_POS_INF, jnp.int32)))
    thr = _key_to_f32(lo)
    n_gt = count(lambda blk, b: blk > thr)

    def refine_cond(carry):
        return jnp.max(jnp.where(carry[1] >= topk, 1, 0)) > 0

    def refine_body(carry):
        cur, cnt = carry
        above = reduce_keys(lambda blk, b: jnp.where(blk > cur, blk, jnp.inf), jnp.inf,
                            lambda a, m: jnp.minimum(a, jnp.min(m, axis=0)))
        cur = jnp.where(cnt >= topk, jnp.min(above, axis=0, keepdims=True), cur)
        return cur, count(lambda blk, b: blk > cur)

    thr, n_gt = lax.while_loop(refine_cond, refine_body, (thr, n_gt))
    need = topk - n_gt
    n_eq = count(lambda blk, b: blk == thr)
    has_tie = jnp.logical_and(n_eq > need, thr > -jnp.inf)
    j_max = nkb * kb - 1
    j_scr[...] = jnp.full((8, tq), j_max, jnp.int32)

    @pl.when(jnp.max(jnp.where(has_tie, 1, 0)) > 0)
    def _():
        def tie_body(_, carry):
            tlo, thi = carry
            mid = (tlo + thi) >> 1
            ok = count(lambda blk, b: jnp.logical_and(blk == thr, key0 + b * kb <= mid)) >= need
            return jnp.where(ok, tlo, mid + 1), jnp.where(ok, mid, thi)
        tlo, _ = lax.fori_loop(0, n_tie_iter, tie_body,
                               (jnp.zeros((1, tq), jnp.int32), jnp.full((1, tq), j_max, jnp.int32)))
        j_scr[...] = jnp.broadcast_to(jnp.where(has_tie, tlo, j_max), (8, tq))

    j_sel = j_scr[0:1, :]

    qall = q_ref[...].reshape(rows, ATT_HEAD_DIM)
    m_scr[...] = jnp.full((rows, half), NEG_BIG, F32)

    def logits_body(b, carry):
        kblk = k_ref[pl.ds(pl.multiple_of(b * kb, kb), kb), :]
        blk = idx_scr[b]
        key = key0 + b * kb
        sel = jnp.logical_or(blk > thr, jnp.logical_and(blk == thr, key <= j_sel))
        bias = jnp.where(jnp.logical_and(sel, key < limit), 0.0, NEG_BIG).T
        lg = _nt_dot(qall, kblk).reshape(heads, tq, kb) + bias[None]
        lg_scr[b] = lg
        lg2 = lg.reshape(rows, kb)
        m_scr[...] = jnp.maximum(m_scr[...], jnp.maximum(lg2[:, :half], lg2[:, half:]))
        return carry

    lax.fori_loop(0, nkb, logits_body, 0)
    m_scr[...] = jnp.broadcast_to(jnp.max(m_scr[...], axis=1, keepdims=True), (rows, half))

    l_scr[...] = jnp.zeros((rows, half), F32)
    acc_scr[...] = jnp.zeros((rows, ATT_HEAD_DIM), F32)

    def pv_body(b, carry):
        vblk = v_ref[pl.ds(pl.multiple_of(b * kb, kb), kb), :]
        lg2 = lg_scr[b].reshape(rows, kb)
        mrow = m_scr[...]
        p0 = jnp.exp(lg2[:, :half] - mrow)
        p1 = jnp.exp(lg2[:, half:] - mrow)
        l_scr[...] += p0 + p1
        p = jnp.concatenate([p0, p1], axis=1).astype(BF16)
        acc_scr[...] += jnp.dot(p, vblk, preferred_element_type=F32)
        return carry

    lax.fori_loop(0, nkb, pv_body, 0)
    denom = jnp.sum(l_scr[...], axis=1, keepdims=True)
    o_ref[...] = (acc_scr[...] / denom).reshape(heads, tq, ATT_HEAD_DIM)


def _att_call(q, k, v, qi, ki, wi, *, tq, kb):
    bsz, _, seq, _ = q.shape
    topk = min(TOPK_MAX, seq // TOPK_FRACTION)
    assert tq % CHUNK == 0 and tq >= topk and seq % tq == 0 and seq % kb == 0 and kb % 256 == 0
    nq = seq // tq
    nkb_max = seq // kb
    rows = ATT_HEADS * tq
    kern = functools.partial(_att_kernel, tq=tq, kb=kb, topk=topk, n_tie_iter=max(1, math.ceil(math.log2(seq))))
    return pl.pallas_call(
        kern,
        grid=(bsz, nq),
        in_specs=[
            pl.BlockSpec((None, ATT_HEADS, tq, ATT_HEAD_DIM), lambda b, j: (b, 0, j, 0)),
            pl.BlockSpec((None, seq, ATT_HEAD_DIM), lambda b, j: (b, 0, 0)),
            pl.BlockSpec((None, seq, ATT_HEAD_DIM), lambda b, j: (b, 0, 0)),
            pl.BlockSpec((None, IDX_HEADS, tq, IDX_DIM), lambda b, j: (b, 0, j, 0)),
            pl.BlockSpec((None, seq, IDX_DIM), lambda b, j: (b, 0, 0)),
            pl.BlockSpec((None, IDX_HEADS, tq), lambda b, j: (b, 0, j)),
        ],
        out_specs=pl.BlockSpec((None, ATT_HEADS, tq, ATT_HEAD_DIM), lambda b, j: (b, 0, j, 0)),
        out_shape=jax.ShapeDtypeStruct((bsz, ATT_HEADS, seq, ATT_HEAD_DIM), F32),
        scratch_shapes=[
            pltpu.VMEM((nkb_max, kb, tq), F32),
            pltpu.VMEM((nkb_max, ATT_HEADS, tq, kb), F32),
            pltpu.VMEM((rows, kb // 2), F32),
            pltpu.VMEM((rows, kb // 2), F32),
            pltpu.VMEM((rows, ATT_HEAD_DIM), F32),
            pltpu.VMEM((8, tq), jnp.int32),
        ],
        compiler_params=_cparams(("arbitrary", "arbitrary")),
        name="dsa_attention",
    )(q, k, v, qi, ki, wi)


def _ssm_kernel(u_ref, kbd_ref, bm_ref, cm_ref, a8_ref, d_ref, wglu_ref, o_ref, st_scr, y_scr, u_scr, *, seq):
    nc = seq // SSM_SUB
    half = SSM_LANES // 2
    u = u_ref[...]
    sub = lax.broadcasted_iota(jnp.int32, (seq, 1), 0) % SSM_SUB
    y = u * d_ref[...] + jnp.dot(u.astype(BF16), kbd_ref[0], preferred_element_type=F32)
    for tau in range(1, SSM_SUB):
        shifted = jnp.where(sub >= tau, pltpu.roll(u, tau, 0), 0.0).astype(BF16)
        y = y + jnp.dot(shifted, kbd_ref[tau], preferred_element_type=F32)
    for hf in range(2):
        y_scr[hf] = y[:, hf * 128:(hf + 1) * 128]
        u_scr[hf] = u[:, hf * 128:(hf + 1) * 128]

    e = jnp.zeros((nc, SSM_LANES), F32)
    for s in range(SSM_SUB):
        rows = pl.ds(s, nc, stride=SSM_SUB)
        us = jnp.concatenate([u_scr[0, rows, :], u_scr[1, rows, :]], axis=1).astype(BF16)
        e = e + jnp.dot(us, bm_ref[s], preferred_element_type=F32)
    st_scr[...] = e

    a8r = a8_ref[:, :half]
    a8i = a8_ref[:, half:]

    def scan_body(c, carry):
        xr, xi = carry
        inc = st_scr[pl.ds(c, 1), :]
        st_scr[pl.ds(c, 1), :] = jnp.concatenate([xr, xi], axis=1)
        return (a8r * xr - a8i * xi + inc[:, :half], a8r * xi + a8i * xr + inc[:, half:])

    lax.fori_loop(0, nc, scan_body, (jnp.zeros((1, half), F32), jnp.zeros((1, half), F32)))

    x_in = st_scr[...].astype(BF16)
    for i in range(SSM_SUB):
        rows = pl.ds(i, nc, stride=SSM_SUB)
        yi = jnp.dot(x_in, cm_ref[i], preferred_element_type=F32)
        for hf in range(2):
            y_scr[hf, rows, :] = y_scr[hf, rows, :] + yi[:, hf * 128:(hf + 1) * 128]

    act = jax.nn.gelu(jnp.concatenate([y_scr[0], y_scr[1]], axis=1)).astype(BF16)
    ag = jnp.dot(act, wglu_ref[...], preferred_element_type=F32)
    o_ref[...] = ag[:, :SSM_WIDTH] * jax.nn.sigmoid(ag[:, SSM_WIDTH:])


def _ssm_call(u, kbd, bm, cm, a8, d, w_glu):
    bsz, seq, _ = u.shape
    assert seq % SSM_SUB == 0
    return pl.pallas_call(
        functools.partial(_ssm_kernel, seq=seq),
        grid=(bsz,),
        in_specs=[
            pl.BlockSpec((None, seq, SSM_WIDTH), lambda b: (b, 0, 0)),
            _resident((SSM_SUB, SSM_WIDTH, SSM_WIDTH)),
            _resident((SSM_SUB, SSM_WIDTH, SSM_LANES)),
            _resident((SSM_SUB, SSM_LANES, SSM_WIDTH)),
            _resident((1, SSM_LANES)),
            _resident((1, SSM_WIDTH)),
            _resident((SSM_WIDTH, 2 * SSM_WIDTH)),
        ],
        out_specs=pl.BlockSpec((None, seq, SSM_WIDTH), lambda b: (b, 0, 0)),
        out_shape=jax.ShapeDtypeStruct((bsz, seq, SSM_WIDTH), F32),
        scratch_shapes=[pltpu.VMEM((seq // SSM_SUB, SSM_LANES), F32), pltpu.VMEM((2, seq, 128), F32),
                        pltpu.VMEM((2, seq, 128), F32)],
        compiler_params=_cparams(("arbitrary",)),
        name="s5_ssm",
    )(u, kbd, bm, cm, a8, d, w_glu)


def _ssm_params(a_re, a_im, log_dt, b_re, b_im, c_re, c_im, d_skip):
    dt = jnp.exp(log_dt)[:, None]
    mag = jnp.exp(dt * a_re)
    abr, abi = mag * jnp.cos(dt * a_im), mag * jnp.sin(dt * a_im)
    den = a_re * a_re + a_im * a_im
    nr, ni = abr - 1.0, abi
    qr, qi = (nr * a_re + ni * a_im) / den, (ni * a_re - nr * a_im) / den
    bbr = qr[..., None] * b_re - qi[..., None] * b_im
    bbi = qr[..., None] * b_im + qi[..., None] * b_re
    pr, pi = [jnp.ones_like(abr)], [jnp.zeros_like(abr)]
    for _ in range(SSM_SUB):
        pr, pi = pr + [pr[-1] * abr - pi[-1] * abi], pi + [pr[-1] * abi + pi[-1] * abr]
    pwr, pwi = jnp.stack(pr), jnp.stack(pi)
    eye = jnp.eye(SSM_GROUPS, dtype=F32)
    abr_t = pwr[:SSM_SUB, :, :, None] * bbr[None] - pwi[:SSM_SUB, :, :, None] * bbi[None]
    abi_t = pwr[:SSM_SUB, :, :, None] * bbi[None] + pwi[:SSM_SUB, :, :, None] * bbr[None]
    kt = (jnp.einsum('gop,tgpc->tgco', c_re, abr_t, precision=HI)
          - jnp.einsum('gop,tgpc->tgco', c_im, abi_t, precision=HI))
    kbd = jnp.einsum('tgco,gh->tgcho', kt, eye).reshape(SSM_SUB, SSM_WIDTH, SSM_WIDTH)
    rev_r = jnp.stack([abr_t[SSM_SUB - 1 - s] for s in range(SSM_SUB)])
    rev_i = jnp.stack([abi_t[SSM_SUB - 1 - s] for s in range(SSM_SUB)])
    bm_r = jnp.einsum('sgpc,gh->sgchp', rev_r, eye).reshape(SSM_SUB, SSM_WIDTH, SSM_GROUPS * SSM_STATE)
    bm_i = jnp.einsum('sgpc,gh->sgchp', rev_i, eye).reshape(SSM_SUB, SSM_WIDTH, SSM_GROUPS * SSM_STATE)
    bm = jnp.concatenate([bm_r, bm_i], axis=-1)
    ar1, ai1 = pwr[1:, :, None, :], pwi[1:, :, None, :]
    mr = c_re[None] * ar1 - c_im[None] * ai1
    mi = c_re[None] * ai1 + c_im[None] * ar1
    cm_r = jnp.einsum('igop,gh->igpho', mr, eye).reshape(SSM_SUB, SSM_GROUPS * SSM_STATE, SSM_WIDTH)
    cm_i = jnp.einsum('igop,gh->igpho', -mi, eye).reshape(SSM_SUB, SSM_GROUPS * SSM_STATE, SSM_WIDTH)
    cm = jnp.concatenate([cm_r, cm_i], axis=1)
    a8 = jnp.concatenate([pwr[SSM_SUB].reshape(1, -1), pwi[SSM_SUB].reshape(1, -1)], axis=1)
    return kbd.astype(BF16), bm.astype(BF16), cm.astype(BF16), a8, d_skip.reshape(1, SSM_WIDTH)


def _conv_kernel(cur_ref, halo_ref, wdw_ref, bdw_ref, lng_ref, lnb_ref, wpw_ref, o_ref, pad_scr, *, rb):
    t = pl.program_id(1)

    def glu(blk):
        return blk[:, :CONV_WIDTH] * jax.nn.sigmoid(blk[:, CONV_WIDTH:])

    pad_scr[0:CONV_HALO, :] = glu(halo_ref[...]) * (t > 0).astype(F32)
    pad_scr[CONV_HALO:, :] = glu(cur_ref[...])
    y = jnp.zeros((rb, CONV_WIDTH), F32) + bdw_ref[...]
    for tap in range(CONV_KSIZE):
        off = CONV_HALO - (CONV_KSIZE - 1) + tap
        y = y + pad_scr[off:off + rb, :] * wdw_ref[tap:tap + 1, :]
    mu = jnp.mean(y, axis=-1, keepdims=True)
    var = jnp.mean(jnp.square(y - mu), axis=-1, keepdims=True)
    z = (y - mu) * lax.rsqrt(var + NORM_EPS) * lng_ref[...] + lnb_ref[...]
    z = (z * jax.nn.sigmoid(z)).astype(BF16)
    o_ref[...] = jnp.dot(z, wpw_ref[...], preferred_element_type=F32)


def _conv_call(uc, w_dw, b_dw, ln_g, ln_b, w_pw, *, rb):
    bsz, seq, _ = uc.shape
    assert seq % rb == 0 and rb % CONV_HALO == 0
    per = rb // CONV_HALO
    return pl.pallas_call(
        functools.partial(_conv_kernel, rb=rb),
        grid=(bsz, seq // rb),
        in_specs=[
            pl.BlockSpec((None, rb, 2 * CONV_WIDTH), lambda b, t: (b, t, 0)),
            pl.BlockSpec((None, CONV_HALO, 2 * CONV_WIDTH), lambda b, t: (b, jnp.maximum(t * per - 1, 0), 0)),
            _resident((CONV_KSIZE, CONV_WIDTH)),
            _resident((1, CONV_WIDTH)),
            _resident((1, CONV_WIDTH)),
            _resident((1, CONV_WIDTH)),
            _resident((CONV_WIDTH, CONV_WIDTH)),
        ],
        out_specs=pl.BlockSpec((None, rb, CONV_WIDTH), lambda b, t: (b, t, 0)),
        out_shape=jax.ShapeDtypeStruct((bsz, seq, CONV_WIDTH), F32),
        scratch_shapes=[pltpu.VMEM((rb + CONV_HALO, CONV_WIDTH), F32)],
        compiler_params=_cparams(("arbitrary", "arbitrary")),
        name="conformer_conv",
    )(uc, uc, w_dw, b_dw, ln_g, ln_b, w_pw)


def _merge_kernel(x_ref, mod_ref, ya_ref, ys_ref, yc_ref, ga_ref, gs_ref, gc_ref, wo_ref, o_ref):
    ya = ya_ref[...]
    ss = jnp.sum(jnp.sum(ya * ya, axis=-1, keepdims=True), axis=0)
    ra = lax.rsqrt(ss * (1.0 / ATT_WIDTH) + NORM_EPS)
    acc = jnp.zeros(x_ref.shape, F32)
    for hd in range(ATT_HEADS):
        part = (ya[hd] * ra * ga_ref[hd:hd + 1, :]).astype(BF16)
        acc = acc + jnp.dot(part, wo_ref[hd * ATT_HEAD_DIM:(hd + 1) * ATT_HEAD_DIM, :], preferred_element_type=F32)
    for y_ref, g_ref, r0, width in ((ys_ref, gs_ref, ATT_WIDTH, SSM_WIDTH), (yc_ref, gc_ref, ATT_WIDTH + SSM_WIDTH, CONV_WIDTH)):
        yv = y_ref[...]
        part = (yv * lax.rsqrt(jnp.mean(yv * yv, axis=-1, keepdims=True) + NORM_EPS) * g_ref[...]).astype(BF16)
        acc = acc + jnp.dot(part, wo_ref[r0:r0 + width, :], preferred_element_type=F32)
    o_ref[...] = x_ref[...] + mod_ref[5:6, :] * acc


def _merge_call(x, mod3, y_att, y_ssm, y_conv, g_att, g_ssm, g_conv, w_out, *, tm):
    bsz, seq, _ = x.shape
    tok = lambda width: pl.BlockSpec((None, tm, width), lambda b, t: (b, t, 0))
    return pl.pallas_call(
        _merge_kernel,
        grid=(bsz, seq // tm),
        in_specs=[
            tok(D_MODEL),
            pl.BlockSpec((None, N_MOD, D_MODEL), lambda b, t: (b, 0, 0)),
            pl.BlockSpec((None, ATT_HEADS, tm, ATT_HEAD_DIM), lambda b, t: (b, 0, t, 0)),
            tok(SSM_WIDTH), tok(CONV_WIDTH),
            _resident((ATT_HEADS, ATT_HEAD_DIM)),
            _resident((1, SSM_WIDTH)),
            _resident((1, CONV_WIDTH)),
            _resident((D_MODEL, D_MODEL)),
        ],
        out_specs=tok(D_MODEL),
        out_shape=jax.ShapeDtypeStruct(x.shape, F32),
        compiler_params=_cparams(("arbitrary", "arbitrary")),
        name="mixer_merge",
    )(x, mod3, y_att, y_ssm, y_conv, g_att, g_ssm, g_conv, w_out)


def _reorder_w_in(w_in):
    splits = [0]
    for c in IN_COLS:
        splits.append(splits[-1] + c)
    q, k, v, qi, ki, wi, us, uc = [w_in[:, splits[i]:splits[i + 1]] for i in range(len(IN_COLS))]
    pad = jnp.zeros((w_in.shape[0], PROJ_SSM0 - PROJ_KW0 - IDX_DIM - IDX_HEADS), w_in.dtype)
    return jnp.concatenate([q, k, v, qi, ki, wi, pad, us, uc], axis=1).astype(BF16)


def kernel(x, c, mod_w, mod_b, ffn1_norm, ffn1_w_gu, ffn1_w_down, mix_norm, w_in, w_out, att_out_norm, ssm_out_norm, conv_out_norm, ssm_a_re, ssm_a_im, ssm_log_dt, ssm_b_re, ssm_b_im, ssm_c_re, ssm_c_im, ssm_d, ssm_w_glu, conv_w_dw, conv_b_dw, conv_ln_g, conv_ln_b, conv_w_pw, ffn2_norm, ffn2_w_gu, ffn2_w_down, final_norm):
    bsz, seq, _ = x.shape
    depth = mod_w.shape[0]
    tm = min(512, seq)
    tq = min(256, seq)
    mod = _mod_call(c, mod_w, mod_b).reshape(depth, bsz, N_MOD, D_MODEL)
    row = lambda v: v.reshape(1, -1)
    fin = row(final_norm)
    for l in range(depth):
        mod3 = mod[l]
        x = _ffn_call(x, mod3, row(ffn1_norm[l]), ffn1_w_gu[l].astype(BF16), ffn1_w_down[l].astype(BF16), fin,
                      row0=0, final=False, tm=tm)
        wi0 = sum(IN_COLS[:5])
        w_wi_t = w_in[l][:, wi0:wi0 + IDX_HEADS].T.astype(BF16)
        q, k, v, qi, ki, wi, us, uc = _proj_call(x, mod3, row(mix_norm[l]), _reorder_w_in(w_in[l]), w_wi_t, tm=tm)
        y_att = _att_call(q, k, v, qi, ki, wi, tq=tq, kb=256)
        y_ssm = _ssm_call(us, *_ssm_params(ssm_a_re[l], ssm_a_im[l], ssm_log_dt[l], ssm_b_re[l], ssm_b_im[l],
                                           ssm_c_re[l], ssm_c_im[l], ssm_d[l]), ssm_w_glu[l].astype(BF16))
        y_conv = _conv_call(uc, conv_w_dw[l], row(conv_b_dw[l]), row(conv_ln_g[l]), row(conv_ln_b[l]),
                            conv_w_pw[l].astype(BF16), rb=min(256, seq))
        x = _merge_call(x, mod3, y_att, y_ssm, y_conv, att_out_norm[l].reshape(ATT_HEADS, ATT_HEAD_DIM),
                        row(ssm_out_norm[l]), row(conv_out_norm[l]), w_out[l].astype(BF16), tm=tm)
        x = _ffn_call(x, mod3, row(ffn2_norm[l]), ffn2_w_gu[l].astype(BF16), ffn2_w_down[l].astype(BF16), fin,
                      row0=6, final=(l == depth - 1), tm=tm)
    return x
```

```python
import functools
import math

import jax
import jax.numpy as jnp
from jax import lax
from jax.experimental import pallas as pl
from jax.experimental.pallas import tpu as pltpu

F32 = jnp.float32
BF16 = jnp.bfloat16

D_MODEL = 1024
CHUNK = 64
ATT_HEADS = 8
ATT_HEAD_DIM = 64
ATT_WIDTH = ATT_HEADS * ATT_HEAD_DIM
IDX_HEADS = 8
IDX_DIM = 32
TOPK_MAX = 256
TOPK_FRACTION = 4
SSM_WIDTH = 256
SSM_GROUP = 16
SSM_GROUPS = 16
SSM_STATE = 64
SSM_LANES = 2 * SSM_GROUPS * SSM_STATE
SSM_SUB = 8
CONV_WIDTH = 256
CONV_KSIZE = 31
CONV_HALO = 32
D_FF = 2816
N_MOD = 9
NORM_EPS = 1e-6
IN_COLS = (ATT_WIDTH, ATT_HEAD_DIM, ATT_HEAD_DIM, IDX_HEADS * IDX_DIM, IDX_DIM, IDX_HEADS, SSM_WIDTH, 2 * CONV_WIDTH)

VMEM_LIMIT = 56 * 1024 * 1024
NEG_BIG = -1e30
KEY_NEG_INF = -2139095041
KEY_POS_INF = 2139095040
HI = lax.Precision.HIGHEST


def _cparams(sem):
    return pltpu.CompilerParams(dimension_semantics=sem, vmem_limit_bytes=VMEM_LIMIT)


def _resident(shape):
    nd = len(shape)
    return pl.BlockSpec(shape, lambda *_: (0,) * nd, pipeline_mode=pl.Buffered(1))


def _nt_dot(a, b):
    return lax.dot_general(a, b, (((1,), (1,)), ((), ())), preferred_element_type=F32)


def _modulated_norm(x, gain, shift, scale):
    y = x * lax.rsqrt(jnp.mean(x * x, axis=-1, keepdims=True) + NORM_EPS) * gain
    return y * (1.0 + scale) + shift


def _mod_kernel(c_ref, w_ref, b_ref, o_ref):
    c = c_ref[...]
    cond = c * jax.nn.sigmoid(c)
    o_ref[...] = jnp.dot(cond, w_ref[...], preferred_element_type=F32, precision=HI) + b_ref[...]


def _mod_call(c, mod_w, mod_b):
    depth = mod_w.shape[0]
    bsz = c.shape[0]
    return pl.pallas_call(
        _mod_kernel,
        grid=(depth, N_MOD),
        in_specs=[
            pl.BlockSpec((bsz, D_MODEL), lambda l, j: (0, 0)),
            pl.BlockSpec((None, D_MODEL, D_MODEL), lambda l, j: (l, 0, j)),
            pl.BlockSpec((None, 1, D_MODEL), lambda l, j: (l, 0, j)),
        ],
        out_specs=pl.BlockSpec((None, bsz, D_MODEL), lambda l, j: (l, 0, j)),
        out_shape=jax.ShapeDtypeStruct((depth, bsz, N_MOD * D_MODEL), F32),
        compiler_params=_cparams(("arbitrary", "arbitrary")),
        name="adaln_mod",
    )(c, mod_w, mod_b.reshape(depth, 1, N_MOD * D_MODEL))


FFN_CHUNK = 256


def _rms_gain(y, gain):
    return (y * lax.rsqrt(jnp.mean(y * y, axis=-1, keepdims=True) + NORM_EPS) * gain).astype(BF16)


def _mix_ffn_kernel(x_ref, mod_ref, ya_ref, ys_ref, yc_ref, ga_ref, gs_ref, gc_ref, wo_ref,
                    ng_ref, wgu_ref, wd_ref, fn_ref, o_ref, *, final):
    mix = jnp.dot(_rms_gain(ya_ref[...], ga_ref[...]), wo_ref[0:ATT_WIDTH, :], preferred_element_type=F32)
    mix = mix + jnp.dot(_rms_gain(ys_ref[...], gs_ref[...]), wo_ref[ATT_WIDTH:ATT_WIDTH + SSM_WIDTH, :],
                        preferred_element_type=F32)
    mix = mix + jnp.dot(_rms_gain(yc_ref[...], gc_ref[...]), wo_ref[ATT_WIDTH + SSM_WIDTH:, :],
                        preferred_element_type=F32)
    x = x_ref[...] + mod_ref[5:6, :] * mix
    o_ref[...] = _ffn_math(x, mod_ref, ng_ref, wgu_ref, wd_ref, fn_ref, row0=6, final=final)


def _ffn_kernel(x_ref, mod_ref, ng_ref, wgu_ref, wd_ref, fn_ref, o_ref, *, row0, final):
    o_ref[...] = _ffn_math(x_ref[...], mod_ref, ng_ref, wgu_ref, wd_ref, fn_ref, row0=row0, final=final)


def _ffn_math(x, mod_ref, ng_ref, wgu_ref, wd_ref, fn_ref, *, row0, final):
    h = _modulated_norm(x, ng_ref[...], mod_ref[row0:row0 + 1, :], mod_ref[row0 + 1:row0 + 2, :]).astype(BF16)
    acc = jnp.zeros(x.shape, F32)
    for c0 in range(0, D_FF, FFN_CHUNK):
        gate = jnp.dot(h, wgu_ref[:, c0:c0 + FFN_CHUNK], preferred_element_type=F32)
        up = jnp.dot(h, wgu_ref[:, D_FF + c0:D_FF + c0 + FFN_CHUNK], preferred_element_type=F32)
        a = (gate * jax.nn.sigmoid(gate) * up).astype(BF16)
        acc = acc + jnp.dot(a, wd_ref[c0:c0 + FFN_CHUNK, :], preferred_element_type=F32)
    out = x + (0.5 * mod_ref[row0 + 2:row0 + 3, :]) * acc
    if final:
        out = out * lax.rsqrt(jnp.mean(out * out, axis=-1, keepdims=True) + NORM_EPS) * fn_ref[...]
    return out


def _mix_ffn_call(x, mod3, y_att, y_ssm, y_conv, g_att, g_ssm, g_conv, w_out, norm_g, w_gu, w_down, final_g,
                  *, final, tm):
    bsz, seq, _ = x.shape
    tok = lambda width: pl.BlockSpec((None, tm, width), lambda b, t: (b, t, 0))
    return pl.pallas_call(
        functools.partial(_mix_ffn_kernel, final=final),
        grid=(bsz, seq // tm),
        in_specs=[
            tok(D_MODEL),
            pl.BlockSpec((None, N_MOD, D_MODEL), lambda b, t: (b, 0, 0)),
            tok(ATT_WIDTH), tok(SSM_WIDTH), tok(CONV_WIDTH),
            _resident((1, ATT_WIDTH)), _resident((1, SSM_WIDTH)), _resident((1, CONV_WIDTH)),
            _resident((D_MODEL, D_MODEL)),
            _resident((1, D_MODEL)),
            _resident((D_MODEL, 2 * D_FF)),
            _resident((D_FF, D_MODEL)),
            _resident((1, D_MODEL)),
        ],
        out_specs=tok(D_MODEL),
        out_shape=jax.ShapeDtypeStruct(x.shape, F32),
        compiler_params=_cparams(("arbitrary", "arbitrary")),
        name="mix_ffn_final" if final else "mix_ffn",
    )(x, mod3, y_att, y_ssm, y_conv, g_att, g_ssm, g_conv, w_out, norm_g, w_gu, w_down, final_g)


def _ffn_call(x, mod3, norm_g, w_gu, w_down, final_g, *, row0, final, tm):
    bsz, seq, _ = x.shape
    nt = seq // tm
    return pl.pallas_call(
        functools.partial(_ffn_kernel, row0=row0, final=final),
        grid=(bsz, nt),
        in_specs=[
            pl.BlockSpec((None, tm, D_MODEL), lambda b, t: (b, t, 0)),
            pl.BlockSpec((None, N_MOD, D_MODEL), lambda b, t: (b, 0, 0)),
            _resident((1, D_MODEL)),
            _resident((D_MODEL, 2 * D_FF)),
            _resident((D_FF, D_MODEL)),
            _resident((1, D_MODEL)),
        ],
        out_specs=pl.BlockSpec((None, tm, D_MODEL), lambda b, t: (b, t, 0)),
        out_shape=jax.ShapeDtypeStruct(x.shape, F32),
        compiler_params=_cparams(("arbitrary", "arbitrary")),
        name="ffn_final" if final else "ffn",
    )(x, mod3, norm_g, w_gu, w_down, final_g)


PROJ_Q0, PROJ_KV0, PROJ_QI0, PROJ_KW0, PROJ_SSM0, PROJ_CONV0, PROJ_END = 0, 512, 640, 896, 1024, 1280, 1792


def _proj_kernel(x_ref, mod_ref, ng_ref, w_ref, wit_ref, q_ref, k_ref, v_ref, qi_ref, ki_ref, wi_ref, us_ref, uc_ref):
    x = x_ref[...]
    h = _modulated_norm(x, ng_ref[...], mod_ref[3:4, :], mod_ref[4:5, :]).astype(BF16)
    q = jnp.dot(h, w_ref[:, PROJ_Q0:PROJ_KV0], preferred_element_type=F32) * (ATT_HEAD_DIM ** -0.5)
    for hd in range(ATT_HEADS):
        q_ref[hd] = q[:, hd * ATT_HEAD_DIM:(hd + 1) * ATT_HEAD_DIM].astype(BF16)
    kv = jnp.dot(h, w_ref[:, PROJ_KV0:PROJ_QI0], preferred_element_type=F32)
    k_ref[...] = kv[:, :ATT_HEAD_DIM].astype(BF16)
    v_ref[...] = kv[:, ATT_HEAD_DIM:].astype(BF16)
    qi = jnp.dot(h, w_ref[:, PROJ_QI0:PROJ_KW0], preferred_element_type=F32)
    for hd in range(IDX_HEADS):
        qi_ref[hd] = qi[:, hd * IDX_DIM:(hd + 1) * IDX_DIM].astype(BF16)
    kw = jnp.dot(h, w_ref[:, PROJ_KW0:PROJ_SSM0], preferred_element_type=F32)
    ki_ref[...] = kw[:, :IDX_DIM].astype(BF16)
    wi_ref[...] = _nt_dot(wit_ref[...], h) * (IDX_DIM ** -0.5 * IDX_HEADS ** -0.5)
    us_ref[...] = jnp.dot(h, w_ref[:, PROJ_SSM0:PROJ_CONV0], preferred_element_type=F32)
    uc_ref[...] = jnp.dot(h, w_ref[:, PROJ_CONV0:PROJ_END], preferred_element_type=F32)


def _proj_call(x, mod3, norm_g, w_in_r, w_wi_t, *, tm):
    bsz, seq, _ = x.shape
    nt = seq // tm
    tok = lambda width: pl.BlockSpec((None, tm, width), lambda b, t: (b, t, 0))
    head = lambda n, width: pl.BlockSpec((None, n, tm, width), lambda b, t: (b, 0, t, 0))
    return pl.pallas_call(
        _proj_kernel,
        grid=(bsz, nt),
        in_specs=[
            tok(D_MODEL),
            pl.BlockSpec((None, N_MOD, D_MODEL), lambda b, t: (b, 0, 0)),
            _resident((1, D_MODEL)),
            _resident((D_MODEL, PROJ_END)),
            _resident((IDX_HEADS, D_MODEL)),
        ],
        out_specs=[
            head(ATT_HEADS, ATT_HEAD_DIM), tok(ATT_HEAD_DIM), tok(ATT_HEAD_DIM),
            head(IDX_HEADS, IDX_DIM), tok(IDX_DIM),
            pl.BlockSpec((None, IDX_HEADS, tm), lambda b, t: (b, 0, t)),
            tok(SSM_WIDTH), tok(2 * CONV_WIDTH),
        ],
        out_shape=[
            jax.ShapeDtypeStruct((bsz, ATT_HEADS, seq, ATT_HEAD_DIM), BF16),
            jax.ShapeDtypeStruct((bsz, seq, ATT_HEAD_DIM), BF16),
            jax.ShapeDtypeStruct((bsz, seq, ATT_HEAD_DIM), BF16),
            jax.ShapeDtypeStruct((bsz, IDX_HEADS, seq, IDX_DIM), BF16),
            jax.ShapeDtypeStruct((bsz, seq, IDX_DIM), BF16),
            jax.ShapeDtypeStruct((bsz, IDX_HEADS, seq), F32),
            jax.ShapeDtypeStruct((bsz, seq, SSM_WIDTH), F32),
            jax.ShapeDtypeStruct((bsz, seq, 2 * CONV_WIDTH), F32),
        ],
        compiler_params=_cparams(("arbitrary", "arbitrary")),
        name="mixer_in_proj",
    )(x, mod3, norm_g, w_in_r, w_wi_t)


def _key_to_f32(key):
    return lax.bitcast_convert_type(jnp.where(key >= 0, key, key ^ 0x7FFFFFFF), F32)


def _att_kernel(q_ref, k_ref, v_ref, qi_ref, ki_ref, wi_ref, o_ref,
                idx_scr, xb_scr, lg_scr, m_scr, l_scr, acc_scr, j_scr, *, tq, kb, topk, n_tie_iter):
    j = pl.program_id(1)
    nkb = (j * tq + tq + kb - 1) // kb
    heads = ATT_HEADS
    rows = heads * tq
    half = kb // 2

    qcol = lax.broadcasted_iota(jnp.int32, (1, tq), 1)
    limit = j * tq + ((qcol // CHUNK) + 1) * CHUNK
    key0 = lax.broadcasted_iota(jnp.int32, (kb, tq), 0)
    wt = wi_ref[...]

    def idx_body(b, carry):
        kib = ki_ref[pl.ds(pl.multiple_of(b * kb, kb), kb), :]
        acc = jnp.zeros((kb, tq), F32)
        for hd in range(IDX_HEADS):
            acc = acc + jnp.maximum(_nt_dot(kib, qi_ref[hd]), 0.0) * wt[hd:hd + 1, :]
        score = jnp.where(key0 + b * kb < limit, acc, -jnp.inf)
        idx_scr[b] = score
        xb_scr[b] = score.astype(BF16)
        return carry

    lax.fori_loop(0, nkb, idx_body, 0)

    def reduce_keys(fn, init, combine):
        def body(b, acc):
            return combine(acc, fn(idx_scr[b], b).reshape(kb // 8, 8, tq))
        return lax.fori_loop(0, nkb, body, jnp.full((8, tq), init, F32))

    def count(pred):
        acc = reduce_keys(lambda blk, b: jnp.where(pred(blk, b), 1.0, 0.0), 0.0, lambda a, m: a + jnp.sum(m, axis=0))
        return jnp.sum(acc, axis=0, keepdims=True)

    one_b, zero_b = jnp.ones((16, tq), BF16), jnp.zeros((16, tq), BF16)

    def count16(cand_b):
        def body(b, acc):
            blk = xb_scr[b]
            for i in range(kb // 16):
                acc = acc + jnp.where(blk[16 * i:16 * (i + 1)] >= cand_b, one_b, zero_b)
            return acc
        acc = lax.fori_loop(0, nkb, body, zero_b)
        return jnp.sum(acc.astype(F32), axis=0, keepdims=True)

    def coarse_body(_, carry):
        lo, hi = carry
        mid = (lo + hi + 1) >> 1
        pattern = jnp.where(mid >= 0, mid, mid ^ 0x7FFF) << 16
        cand = lax.bitcast_convert_type(pattern, F32).astype(BF16)
        ok = count16(jnp.broadcast_to(cand, (16, tq))) >= topk
        return jnp.where(ok, mid, lo), jnp.where(ok, hi, mid - 1)

    c16, _ = lax.fori_loop(0, 16, coarse_body,
                           (jnp.full((1, tq), KEY_NEG_INF >> 16, jnp.int32), jnp.full((1, tq), KEY_POS_INF >> 16, jnp.int32)))
    key_t16 = jnp.where(c16 >= 0, c16 << 16, (c16 << 16) | 0xFFFF)

    def search_body(_, carry):
        lo, hi = carry
        mid = (lo | hi) - ((lo ^ hi) >> 1)
        cand = _key_to_f32(mid)
        ok = count(lambda blk, b: blk >= cand) >= topk
        return jnp.where(ok, mid, lo), jnp.where(ok, hi, mid - 1)

    lo, _ = lax.fori_loop(0, 17, search_body,
                          (jnp.maximum(key_t16 - 32770, KEY_NEG_INF), jnp.minimum(key_t16 + 65536, KEY_POS_INF)))
    thr = _key_to_f32(lo)
    n_gt = count(lambda blk, b: blk > thr)

    def refine_cond(carry):
        return jnp.max(jnp.where(carry[1] >= topk, 1, 0)) > 0

    def refine_body(carry):
        cur, cnt = carry
        above = reduce_keys(lambda blk, b: jnp.where(blk > cur, blk, jnp.inf), jnp.inf,
                            lambda a, m: jnp.minimum(a, jnp.min(m, axis=0)))
        cur = jnp.where(cnt >= topk, jnp.min(above, axis=0, keepdims=True), cur)
        return cur, count(lambda blk, b: blk > cur)

    thr, n_gt = lax.while_loop(refine_cond, refine_body, (thr, n_gt))
    need = topk - n_gt
    n_eq = count(lambda blk, b: blk == thr)
    has_tie = jnp.logical_and(n_eq > need, thr > -jnp.inf)
    j_max = nkb * kb - 1
    j_scr[...] = jnp.full((8, tq), j_max, jnp.int32)

    @pl.when(jnp.max(jnp.where(has_tie, 1, 0)) > 0)
    def _():
        def tie_body(_, carry):
            tlo, thi = carry
            mid = (tlo + thi) >> 1
            ok = count(lambda blk, b: jnp.logical_and(blk == thr, key0 + b * kb <= mid)) >= need
            return jnp.where(ok, tlo, mid + 1), jnp.where(ok, mid, thi)
        tlo, _ = lax.fori_loop(0, n_tie_iter, tie_body,
                               (jnp.zeros((1, tq), jnp.int32), jnp.full((1, tq), j_max, jnp.int32)))
        j_scr[...] = jnp.broadcast_to(jnp.where(has_tie, tlo, j_max), (8, tq))

    j_sel = j_scr[0:1, :]

    qall = q_ref[...].reshape(rows, ATT_HEAD_DIM)
    m_scr[...] = jnp.full((rows, half), NEG_BIG, F32)

    def logits_body(b, carry):
        kblk = k_ref[pl.ds(pl.multiple_of(b * kb, kb), kb), :]
        blk = idx_scr[b]
        key = key0 + b * kb
        sel = jnp.logical_or(blk > thr, jnp.logical_and(blk == thr, key <= j_sel))
        bias = jnp.where(jnp.logical_and(sel, key < limit), 0.0, NEG_BIG).T
        lg = _nt_dot(qall, kblk).reshape(heads, tq, kb) + bias[None]
        lg_scr[b] = lg
        lg2 = lg.reshape(rows, kb)
        m_scr[...] = jnp.maximum(m_scr[...], jnp.maximum(lg2[:, :half], lg2[:, half:]))
        return carry

    lax.fori_loop(0, nkb, logits_body, 0)
    m_scr[...] = jnp.broadcast_to(jnp.max(m_scr[...], axis=1, keepdims=True), (rows, half))

    l_scr[...] = jnp.zeros((rows, half), F32)
    acc_scr[...] = jnp.zeros((rows, ATT_HEAD_DIM), F32)

    def pv_body(b, carry):
        vblk = v_ref[pl.ds(pl.multiple_of(b * kb, kb), kb), :]
        lg2 = lg_scr[b].reshape(rows, kb)
        mrow = m_scr[...]
        p0 = jnp.exp(lg2[:, :half] - mrow)
        p1 = jnp.exp(lg2[:, half:] - mrow)
        l_scr[...] += p0 + p1
        p = jnp.concatenate([p0, p1], axis=1).astype(BF16)
        acc_scr[...] += jnp.dot(p, vblk, preferred_element_type=F32)
        return carry

    lax.fori_loop(0, nkb, pv_body, 0)
    denom = jnp.sum(l_scr[...], axis=1, keepdims=True)
    out = acc_scr[...] / denom
    for hd in range(heads):
        o_ref[:, hd * ATT_HEAD_DIM:(hd + 1) * ATT_HEAD_DIM] = out[hd * tq:(hd + 1) * tq]


def _att_call(q, k, v, qi, ki, wi, *, tq, kb):
    bsz, _, seq, _ = q.shape
    topk = min(TOPK_MAX, seq // TOPK_FRACTION)
    assert tq % CHUNK == 0 and tq >= topk and seq % tq == 0 and seq % kb == 0 and kb % 256 == 0
    nq = seq // tq
    nkb_max = seq // kb
    assert (kb // 16) * nkb_max <= 256
    rows = ATT_HEADS * tq
    kern = functools.partial(_att_kernel, tq=tq, kb=kb, topk=topk, n_tie_iter=max(1, math.ceil(math.log2(seq))))
    return pl.pallas_call(
        kern,
        grid=(bsz, nq),
        in_specs=[
            pl.BlockSpec((None, ATT_HEADS, tq, ATT_HEAD_DIM), lambda b, j: (b, 0, j, 0)),
            pl.BlockSpec((None, seq, ATT_HEAD_DIM), lambda b, j: (b, 0, 0)),
            pl.BlockSpec((None, seq, ATT_HEAD_DIM), lambda b, j: (b, 0, 0)),
            pl.BlockSpec((None, IDX_HEADS, tq, IDX_DIM), lambda b, j: (b, 0, j, 0)),
            pl.BlockSpec((None, seq, IDX_DIM), lambda b, j: (b, 0, 0)),
            pl.BlockSpec((None, IDX_HEADS, tq), lambda b, j: (b, 0, j)),
        ],
        out_specs=pl.BlockSpec((None, tq, ATT_WIDTH), lambda b, j: (b, j, 0)),
        out_shape=jax.ShapeDtypeStruct((bsz, seq, ATT_WIDTH), F32),
        scratch_shapes=[
            pltpu.VMEM((nkb_max, kb, tq), F32),
            pltpu.VMEM((nkb_max, kb, tq), BF16),
            pltpu.VMEM((nkb_max, ATT_HEADS, tq, kb), F32),
            pltpu.VMEM((rows, kb // 2), F32),
            pltpu.VMEM((rows, kb // 2), F32),
            pltpu.VMEM((rows, ATT_HEAD_DIM), F32),
            pltpu.VMEM((8, tq), jnp.int32),
        ],
        compiler_params=_cparams(("arbitrary", "arbitrary")),
        name="dsa_attention",
    )(q, k, v, qi, ki, wi)


def _ssm_kernel(u_ref, kbd_ref, bm_ref, cm_ref, a8_ref, d_ref, wglu_ref, o_ref, st_scr, y_scr, u_scr, *, seq):
    nc = seq // SSM_SUB
    half = SSM_LANES // 2
    u = u_ref[...]
    sub = lax.broadcasted_iota(jnp.int32, (seq, 1), 0) % SSM_SUB
    y = u * d_ref[...] + jnp.dot(u.astype(BF16), kbd_ref[0], preferred_element_type=F32)
    for tau in range(1, SSM_SUB):
        shifted = jnp.where(sub >= tau, pltpu.roll(u, tau, 0), 0.0).astype(BF16)
        y = y + jnp.dot(shifted, kbd_ref[tau], preferred_element_type=F32)
    for hf in range(2):
        y_scr[hf] = y[:, hf * 128:(hf + 1) * 128]
        u_scr[hf] = u[:, hf * 128:(hf + 1) * 128]

    e = jnp.zeros((nc, SSM_LANES), F32)
    for s in range(SSM_SUB):
        rows = pl.ds(s, nc, stride=SSM_SUB)
        us = jnp.concatenate([u_scr[0, rows, :], u_scr[1, rows, :]], axis=1).astype(BF16)
        e = e + jnp.dot(us, bm_ref[s], preferred_element_type=F32)
    st_scr[...] = e

    a8r = a8_ref[:, :half]
    a8i = a8_ref[:, half:]

    def scan_body(c, carry):
        xr, xi = carry
        inc = st_scr[pl.ds(c, 1), :]
        st_scr[pl.ds(c, 1), :] = jnp.concatenate([xr, xi], axis=1)
        return (a8r * xr - a8i * xi + inc[:, :half], a8r * xi + a8i * xr + inc[:, half:])

    lax.fori_loop(0, nc, scan_body, (jnp.zeros((1, half), F32), jnp.zeros((1, half), F32)))

    x_in = st_scr[...].astype(BF16)
    for i in range(SSM_SUB):
        rows = pl.ds(i, nc, stride=SSM_SUB)
        yi = jnp.dot(x_in, cm_ref[i], preferred_element_type=F32)
        for hf in range(2):
            y_scr[hf, rows, :] = y_scr[hf, rows, :] + yi[:, hf * 128:(hf + 1) * 128]

    act = jax.nn.gelu(jnp.concatenate([y_scr[0], y_scr[1]], axis=1)).astype(BF16)
    ag = jnp.dot(act, wglu_ref[...], preferred_element_type=F32)
    o_ref[...] = ag[:, :SSM_WIDTH] * jax.nn.sigmoid(ag[:, SSM_WIDTH:])


def _ssm_call(u, kbd, bm, cm, a8, d, w_glu):
    bsz, seq, _ = u.shape
    assert seq % SSM_SUB == 0
    return pl.pallas_call(
        functools.partial(_ssm_kernel, seq=seq),
        grid=(bsz,),
        in_specs=[
            pl.BlockSpec((None, seq, SSM_WIDTH), lambda b: (b, 0, 0)),
            _resident((SSM_SUB, SSM_WIDTH, SSM_WIDTH)),
            _resident((SSM_SUB, SSM_WIDTH, SSM_LANES)),
            _resident((SSM_SUB, SSM_LANES, SSM_WIDTH)),
            _resident((1, SSM_LANES)),
            _resident((1, SSM_WIDTH)),
            _resident((SSM_WIDTH, 2 * SSM_WIDTH)),
        ],
        out_specs=pl.BlockSpec((None, seq, SSM_WIDTH), lambda b: (b, 0, 0)),
        out_shape=jax.ShapeDtypeStruct((bsz, seq, SSM_WIDTH), F32),
        scratch_shapes=[pltpu.VMEM((seq // SSM_SUB, SSM_LANES), F32), pltpu.VMEM((2, seq, 128), F32),
                        pltpu.VMEM((2, seq, 128), F32)],
        compiler_params=_cparams(("arbitrary",)),
        name="s5_ssm",
    )(u, kbd, bm, cm, a8, d, w_glu)


def _ssm_params(a_re, a_im, log_dt, b_re, b_im, c_re, c_im, d_skip):
    dt = jnp.exp(log_dt)[:, None]
    mag = jnp.exp(dt * a_re)
    abr, abi = mag * jnp.cos(dt * a_im), mag * jnp.sin(dt * a_im)
    den = a_re * a_re + a_im * a_im
    nr, ni = abr - 1.0, abi
    qr, qi = (nr * a_re + ni * a_im) / den, (ni * a_re - nr * a_im) / den
    bbr = qr[..., None] * b_re - qi[..., None] * b_im
    bbi = qr[..., None] * b_im + qi[..., None] * b_re
    pr, pi = [jnp.ones_like(abr)], [jnp.zeros_like(abr)]
    for _ in range(SSM_SUB):
        pr, pi = pr + [pr[-1] * abr - pi[-1] * abi], pi + [pr[-1] * abi + pi[-1] * abr]
    pwr, pwi = jnp.stack(pr), jnp.stack(pi)
    eye = jnp.eye(SSM_GROUPS, dtype=F32)
    abr_t = pwr[:SSM_SUB, :, :, None] * bbr[None] - pwi[:SSM_SUB, :, :, None] * bbi[None]
    abi_t = pwr[:SSM_SUB, :, :, None] * bbi[None] + pwi[:SSM_SUB, :, :, None] * bbr[None]
    kt = (jnp.einsum('gop,tgpc->tgco', c_re, abr_t, precision=HI)
          - jnp.einsum('gop,tgpc->tgco', c_im, abi_t, precision=HI))
    kbd = jnp.einsum('tgco,gh->tgcho', kt, eye).reshape(SSM_SUB, SSM_WIDTH, SSM_WIDTH)
    rev_r = jnp.stack([abr_t[SSM_SUB - 1 - s] for s in range(SSM_SUB)])
    rev_i = jnp.stack([abi_t[SSM_SUB - 1 - s] for s in range(SSM_SUB)])
    bm_r = jnp.einsum('sgpc,gh->sgchp', rev_r, eye).reshape(SSM_SUB, SSM_WIDTH, SSM_GROUPS * SSM_STATE)
    bm_i = jnp.einsum('sgpc,gh->sgchp', rev_i, eye).reshape(SSM_SUB, SSM_WIDTH, SSM_GROUPS * SSM_STATE)
    bm = jnp.concatenate([bm_r, bm_i], axis=-1)
    ar1, ai1 = pwr[1:, :, None, :], pwi[1:, :, None, :]
    mr = c_re[None] * ar1 - c_im[None] * ai1
    mi = c_re[None] * ai1 + c_im[None] * ar1
    cm_r = jnp.einsum('igop,gh->igpho', mr, eye).reshape(SSM_SUB, SSM_GROUPS * SSM_STATE, SSM_WIDTH)
    cm_i = jnp.einsum('igop,gh->igpho', -mi, eye).reshape(SSM_SUB, SSM_GROUPS * SSM_STATE, SSM_WIDTH)
    cm = jnp.concatenate([cm_r, cm_i], axis=1)
    a8 = jnp.concatenate([pwr[SSM_SUB].reshape(1, -1), pwi[SSM_SUB].reshape(1, -1)], axis=1)
    return kbd.astype(BF16), bm.astype(BF16), cm.astype(BF16), a8, d_skip.reshape(1, SSM_WIDTH)


def _conv_kernel(cur_ref, halo_ref, wdw_ref, bdw_ref, lng_ref, lnb_ref, wpw_ref, o_ref, pad_scr, *, rb):
    t = pl.program_id(1)

    def glu(blk):
        return blk[:, :CONV_WIDTH] * jax.nn.sigmoid(blk[:, CONV_WIDTH:])

    pad_scr[0:CONV_HALO, :] = glu(halo_ref[...]) * (t > 0).astype(F32)
    pad_scr[CONV_HALO:CONV_HALO + rb, :] = glu(cur_ref[...])
    pad_scr[CONV_HALO + rb:, :] = jnp.zeros((8, CONV_WIDTH), F32)
    y = jnp.zeros((rb, CONV_WIDTH), F32) + bdw_ref[...]
    for phase in range(8):
        part = None
        for tap in range(CONV_KSIZE):
            off = CONV_HALO - (CONV_KSIZE - 1) + tap
            if off % 8 != phase:
                continue
            term = pad_scr[off - phase:off - phase + rb + 8, :] * wdw_ref[tap:tap + 1, :]
            part = term if part is None else part + term
        y = y + part[phase:phase + rb, :]
    mu = jnp.mean(y, axis=-1, keepdims=True)
    var = jnp.mean(jnp.square(y - mu), axis=-1, keepdims=True)
    z = (y - mu) * lax.rsqrt(var + NORM_EPS) * lng_ref[...] + lnb_ref[...]
    z = (z * jax.nn.sigmoid(z)).astype(BF16)
    o_ref[...] = jnp.dot(z, wpw_ref[...], preferred_element_type=F32)


def _conv_call(uc, w_dw, b_dw, ln_g, ln_b, w_pw, *, rb):
    bsz, seq, _ = uc.shape
    assert seq % rb == 0 and rb % CONV_HALO == 0
    per = rb // CONV_HALO
    return pl.pallas_call(
        functools.partial(_conv_kernel, rb=rb),
        grid=(bsz, seq // rb),
        in_specs=[
            pl.BlockSpec((None, rb, 2 * CONV_WIDTH), lambda b, t: (b, t, 0)),
            pl.BlockSpec((None, CONV_HALO, 2 * CONV_WIDTH), lambda b, t: (b, jnp.maximum(t * per - 1, 0), 0)),
            _resident((CONV_KSIZE, CONV_WIDTH)),
            _resident((1, CONV_WIDTH)),
            _resident((1, CONV_WIDTH)),
            _resident((1, CONV_WIDTH)),
            _resident((CONV_WIDTH, CONV_WIDTH)),
        ],
        out_specs=pl.BlockSpec((None, rb, CONV_WIDTH), lambda b, t: (b, t, 0)),
        out_shape=jax.ShapeDtypeStruct((bsz, seq, CONV_WIDTH), F32),
        scratch_shapes=[pltpu.VMEM((rb + CONV_HALO + 8, CONV_WIDTH), F32)],
        compiler_params=_cparams(("arbitrary", "arbitrary")),
        name="conformer_conv",
    )(uc, uc, w_dw, b_dw, ln_g, ln_b, w_pw)


def _reorder_w_in(w_in):
    splits = [0]
    for c in IN_COLS:
        splits.append(splits[-1] + c)
    q, k, v, qi, ki, wi, us, uc = [w_in[:, splits[i]:splits[i + 1]] for i in range(len(IN_COLS))]
    pad = jnp.zeros((w_in.shape[0], PROJ_SSM0 - PROJ_KW0 - IDX_DIM - IDX_HEADS), w_in.dtype)
    return jnp.concatenate([q, k, v, qi, ki, wi, pad, us, uc], axis=1).astype(BF16)


def kernel(x, c, mod_w, mod_b, ffn1_norm, ffn1_w_gu, ffn1_w_down, mix_norm, w_in, w_out, att_out_norm, ssm_out_norm, conv_out_norm, ssm_a_re, ssm_a_im, ssm_log_dt, ssm_b_re, ssm_b_im, ssm_c_re, ssm_c_im, ssm_d, ssm_w_glu, conv_w_dw, conv_b_dw, conv_ln_g, conv_ln_b, conv_w_pw, ffn2_norm, ffn2_w_gu, ffn2_w_down, final_norm):
    bsz, seq, _ = x.shape
    depth = mod_w.shape[0]
    tm = min(512, seq)
    tq = min(256, seq)
    mod = _mod_call(c, mod_w, mod_b).reshape(depth, bsz, N_MOD, D_MODEL)
    row = lambda v: v.reshape(1, -1)
    fin = row(final_norm)
    for l in range(depth):
        mod3 = mod[l]
        x = _ffn_call(x, mod3, row(ffn1_norm[l]), ffn1_w_gu[l].astype(BF16), ffn1_w_down[l].astype(BF16), fin,
                      row0=0, final=False, tm=tm)
        wi0 = sum(IN_COLS[:5])
        w_wi_t = w_in[l][:, wi0:wi0 + IDX_HEADS].T.astype(BF16)
        q, k, v, qi, ki, wi, us, uc = _proj_call(x, mod3, row(mix_norm[l]), _reorder_w_in(w_in[l]), w_wi_t, tm=tm)
        y_att = _att_call(q, k, v, qi, ki, wi, tq=tq, kb=256)
        y_ssm = _ssm_call(us, *_ssm_params(ssm_a_re[l], ssm_a_im[l], ssm_log_dt[l], ssm_b_re[l], ssm_b_im[l],
                                           ssm_c_re[l], ssm_c_im[l], ssm_d[l]), ssm_w_glu[l].astype(BF16))
        y_conv = _conv_call(uc, conv_w_dw[l], row(conv_b_dw[l]), row(conv_ln_g[l]), row(conv_ln_b[l]),
                            conv_w_pw[l].astype(BF16), rb=min(256, seq))
        x = _mix_ffn_call(x, mod3, y_att, y_ssm, y_conv, row(att_out_norm[l]), row(ssm_out_norm[l]),
                          row(conv_out_norm[l]), w_out[l].astype(BF16), row(ffn2_norm[l]),
                          ffn2_w_gu[l].astype(BF16), ffn2_w_down[l].astype(BF16), fin,
                          final=(l == depth - 1), tm=tm)
    return x
```

```python
import functools
import math

import jax
import jax.numpy as jnp
from jax import lax
from jax.experimental import pallas as pl
from jax.experimental.pallas import tpu as pltpu

F32 = jnp.float32
BF16 = jnp.bfloat16

D_MODEL = 1024
CHUNK = 64
ATT_HEADS = 8
ATT_HEAD_DIM = 64
ATT_WIDTH = ATT_HEADS * ATT_HEAD_DIM
IDX_HEADS = 8
IDX_DIM = 32
TOPK_MAX = 256
TOPK_FRACTION = 4
SSM_WIDTH = 256
SSM_GROUP = 16
SSM_GROUPS = 16
SSM_STATE = 64
SSM_LANES = 2 * SSM_GROUPS * SSM_STATE
SSM_SUB = 8
CONV_WIDTH = 256
CONV_KSIZE = 31
CONV_HALO = 32
D_FF = 2816
N_MOD = 9
NORM_EPS = 1e-6
IN_COLS = (ATT_WIDTH, ATT_HEAD_DIM, ATT_HEAD_DIM, IDX_HEADS * IDX_DIM, IDX_DIM, IDX_HEADS, SSM_WIDTH, 2 * CONV_WIDTH)

VMEM_LIMIT = 56 * 1024 * 1024
NEG_BIG = -1e30
LOG2_E = 1.4426950408889634
KEY_NEG_INF = -2139095041
KEY_POS_INF = 2139095040
HI = lax.Precision.HIGHEST


def _cparams(sem):
    return pltpu.CompilerParams(dimension_semantics=sem, vmem_limit_bytes=VMEM_LIMIT)


def _resident(shape):
    nd = len(shape)
    return pl.BlockSpec(shape, lambda *_: (0,) * nd, pipeline_mode=pl.Buffered(1))


def _nt_dot(a, b):
    return lax.dot_general(a, b, (((1,), (1,)), ((), ())), preferred_element_type=F32)


def _modulated_norm(x, gain, shift, scale):
    y = x * lax.rsqrt(jnp.mean(x * x, axis=-1, keepdims=True) + NORM_EPS) * gain
    return y * (1.0 + scale) + shift


def _mod_kernel(c_ref, w_ref, b_ref, o_ref):
    c = c_ref[...]
    cond = c * jax.nn.sigmoid(c)
    o_ref[...] = jnp.dot(cond, w_ref[...], preferred_element_type=F32, precision=HI) + b_ref[...]


def _mod_call(c, mod_w, mod_b):
    depth = mod_w.shape[0]
    bsz = c.shape[0]
    return pl.pallas_call(
        _mod_kernel,
        grid=(depth, N_MOD),
        in_specs=[
            pl.BlockSpec((bsz, D_MODEL), lambda l, j: (0, 0)),
            pl.BlockSpec((None, D_MODEL, D_MODEL), lambda l, j: (l, 0, j)),
            pl.BlockSpec((None, 1, D_MODEL), lambda l, j: (l, 0, j)),
        ],
        out_specs=pl.BlockSpec((None, bsz, D_MODEL), lambda l, j: (l, 0, j)),
        out_shape=jax.ShapeDtypeStruct((depth, bsz, N_MOD * D_MODEL), F32),
        compiler_params=_cparams(("arbitrary", "arbitrary")),
        name="adaln_mod",
    )(c, mod_w, mod_b.reshape(depth, 1, N_MOD * D_MODEL))


FFN_CHUNK = 256


def _rms_gain(y, gain):
    return (y * lax.rsqrt(jnp.mean(y * y, axis=-1, keepdims=True) + NORM_EPS) * gain).astype(BF16)


def _mix_ffn_kernel(x_ref, mod_ref, ya_ref, ys_ref, yc_ref, ga_ref, gs_ref, gc_ref, wo_ref,
                    ng_ref, wgu_ref, wd_ref, fn_ref, o_ref, *, final):
    mix = jnp.dot(_rms_gain(ya_ref[...], ga_ref[...]), wo_ref[0:ATT_WIDTH, :], preferred_element_type=F32)
    mix = mix + jnp.dot(_rms_gain(ys_ref[...], gs_ref[...]), wo_ref[ATT_WIDTH:ATT_WIDTH + SSM_WIDTH, :],
                        preferred_element_type=F32)
    mix = mix + jnp.dot(_rms_gain(yc_ref[...], gc_ref[...]), wo_ref[ATT_WIDTH + SSM_WIDTH:, :],
                        preferred_element_type=F32)
    x = x_ref[...] + mod_ref[5:6, :] * mix
    o_ref[...] = _ffn_math(x, mod_ref, ng_ref, wgu_ref, wd_ref, fn_ref, row0=6, final=final)


def _ffn_kernel(x_ref, mod_ref, ng_ref, wgu_ref, wd_ref, fn_ref, o_ref, *, row0, final):
    o_ref[...] = _ffn_math(x_ref[...], mod_ref, ng_ref, wgu_ref, wd_ref, fn_ref, row0=row0, final=final)


def _ffn_math(x, mod_ref, ng_ref, wgu_ref, wd_ref, fn_ref, *, row0, final):
    h = _modulated_norm(x, ng_ref[...], mod_ref[row0:row0 + 1, :], mod_ref[row0 + 1:row0 + 2, :]).astype(BF16)
    acc = jnp.zeros(x.shape, F32)
    for c0 in range(0, D_FF, FFN_CHUNK):
        gate = jnp.dot(h, wgu_ref[:, c0:c0 + FFN_CHUNK], preferred_element_type=F32)
        up = jnp.dot(h, wgu_ref[:, D_FF + c0:D_FF + c0 + FFN_CHUNK], preferred_element_type=F32)
        a = (gate * jax.nn.sigmoid(gate) * up).astype(BF16)
        acc = acc + jnp.dot(a, wd_ref[c0:c0 + FFN_CHUNK, :], preferred_element_type=F32)
    out = x + (0.5 * mod_ref[row0 + 2:row0 + 3, :]) * acc
    if final:
        out = out * lax.rsqrt(jnp.mean(out * out, axis=-1, keepdims=True) + NORM_EPS) * fn_ref[...]
    return out


def _mix_ffn_call(x, mod3, y_att, y_ssm, y_conv, g_att, g_ssm, g_conv, w_out, norm_g, w_gu, w_down, final_g,
                  *, final, tm):
    bsz, seq, _ = x.shape
    tok = lambda width: pl.BlockSpec((None, tm, width), lambda b, t: (b, t, 0))
    return pl.pallas_call(
        functools.partial(_mix_ffn_kernel, final=final),
        grid=(bsz, seq // tm),
        in_specs=[
            tok(D_MODEL),
            pl.BlockSpec((None, N_MOD, D_MODEL), lambda b, t: (b, 0, 0)),
            tok(ATT_WIDTH), tok(SSM_WIDTH), tok(CONV_WIDTH),
            _resident((1, ATT_WIDTH)), _resident((1, SSM_WIDTH)), _resident((1, CONV_WIDTH)),
            _resident((D_MODEL, D_MODEL)),
            _resident((1, D_MODEL)),
            _resident((D_MODEL, 2 * D_FF)),
            _resident((D_FF, D_MODEL)),
            _resident((1, D_MODEL)),
        ],
        out_specs=tok(D_MODEL),
        out_shape=jax.ShapeDtypeStruct(x.shape, F32),
        compiler_params=_cparams(("arbitrary", "arbitrary")),
        name="mix_ffn_final" if final else "mix_ffn",
    )(x, mod3, y_att, y_ssm, y_conv, g_att, g_ssm, g_conv, w_out, norm_g, w_gu, w_down, final_g)


def _ffn_call(x, mod3, norm_g, w_gu, w_down, final_g, *, row0, final, tm):
    bsz, seq, _ = x.shape
    nt = seq // tm
    return pl.pallas_call(
        functools.partial(_ffn_kernel, row0=row0, final=final),
        grid=(bsz, nt),
        in_specs=[
            pl.BlockSpec((None, tm, D_MODEL), lambda b, t: (b, t, 0)),
            pl.BlockSpec((None, N_MOD, D_MODEL), lambda b, t: (b, 0, 0)),
            _resident((1, D_MODEL)),
            _resident((D_MODEL, 2 * D_FF)),
            _resident((D_FF, D_MODEL)),
            _resident((1, D_MODEL)),
        ],
        out_specs=pl.BlockSpec((None, tm, D_MODEL), lambda b, t: (b, t, 0)),
        out_shape=jax.ShapeDtypeStruct(x.shape, F32),
        compiler_params=_cparams(("arbitrary", "arbitrary")),
        name="ffn_final" if final else "ffn",
    )(x, mod3, norm_g, w_gu, w_down, final_g)


PROJ_Q0, PROJ_KV0, PROJ_QI0, PROJ_KW0, PROJ_SSM0, PROJ_CONV0, PROJ_END = 0, 512, 640, 896, 1024, 1280, 1792


def _proj_kernel(x_ref, mod_ref, ng_ref, w_ref, wit_ref, q_ref, k_ref, v_ref, qi_ref, ki_ref, wi_ref, us_ref, uc_ref):
    x = x_ref[...]
    h = _modulated_norm(x, ng_ref[...], mod_ref[3:4, :], mod_ref[4:5, :]).astype(BF16)
    q = jnp.dot(h, w_ref[:, PROJ_Q0:PROJ_KV0], preferred_element_type=F32) * (ATT_HEAD_DIM ** -0.5 * LOG2_E)
    for hd in range(ATT_HEADS):
        q_ref[hd] = q[:, hd * ATT_HEAD_DIM:(hd + 1) * ATT_HEAD_DIM].astype(BF16)
    kv = jnp.dot(h, w_ref[:, PROJ_KV0:PROJ_QI0], preferred_element_type=F32)
    k_ref[...] = kv[:, :ATT_HEAD_DIM].astype(BF16)
    lane = lax.broadcasted_iota(jnp.int32, kv.shape, 1)
    ones_col = jnp.where(lane == ATT_HEAD_DIM, 1.0, 0.0)
    v_ref[...] = jnp.where(lane < ATT_HEAD_DIM, pltpu.roll(kv, ATT_HEAD_DIM, 1), ones_col).astype(BF16)
    qi = jnp.dot(h, w_ref[:, PROJ_QI0:PROJ_KW0], preferred_element_type=F32)
    for hd in range(IDX_HEADS):
        qi_ref[hd] = qi[:, hd * IDX_DIM:(hd + 1) * IDX_DIM].astype(BF16)
    kw = jnp.dot(h, w_ref[:, PROJ_KW0:PROJ_SSM0], preferred_element_type=F32)
    ki_ref[...] = kw[:, :IDX_DIM].astype(BF16)
    wi_ref[...] = _nt_dot(wit_ref[...], h) * (IDX_DIM ** -0.5 * IDX_HEADS ** -0.5)
    us_ref[...] = jnp.dot(h, w_ref[:, PROJ_SSM0:PROJ_CONV0], preferred_element_type=F32)
    uc_ref[...] = jnp.dot(h, w_ref[:, PROJ_CONV0:PROJ_END], preferred_element_type=F32)


def _proj_call(x, mod3, norm_g, w_in_r, w_wi_t, *, tm):
    bsz, seq, _ = x.shape
    nt = seq // tm
    tok = lambda width: pl.BlockSpec((None, tm, width), lambda b, t: (b, t, 0))
    head = lambda n, width: pl.BlockSpec((None, n, tm, width), lambda b, t: (b, 0, t, 0))
    return pl.pallas_call(
        _proj_kernel,
        grid=(bsz, nt),
        in_specs=[
            tok(D_MODEL),
            pl.BlockSpec((None, N_MOD, D_MODEL), lambda b, t: (b, 0, 0)),
            _resident((1, D_MODEL)),
            _resident((D_MODEL, PROJ_END)),
            _resident((IDX_HEADS, D_MODEL)),
        ],
        out_specs=[
            head(ATT_HEADS, ATT_HEAD_DIM), tok(ATT_HEAD_DIM), tok(2 * ATT_HEAD_DIM),
            head(IDX_HEADS, IDX_DIM), tok(IDX_DIM),
            pl.BlockSpec((None, IDX_HEADS, tm), lambda b, t: (b, 0, t)),
            tok(SSM_WIDTH), tok(2 * CONV_WIDTH),
        ],
        out_shape=[
            jax.ShapeDtypeStruct((bsz, ATT_HEADS, seq, ATT_HEAD_DIM), BF16),
            jax.ShapeDtypeStruct((bsz, seq, ATT_HEAD_DIM), BF16),
            jax.ShapeDtypeStruct((bsz, seq, 2 * ATT_HEAD_DIM), BF16),
            jax.ShapeDtypeStruct((bsz, IDX_HEADS, seq, IDX_DIM), BF16),
            jax.ShapeDtypeStruct((bsz, seq, IDX_DIM), BF16),
            jax.ShapeDtypeStruct((bsz, IDX_HEADS, seq), F32),
            jax.ShapeDtypeStruct((bsz, seq, SSM_WIDTH), F32),
            jax.ShapeDtypeStruct((bsz, seq, 2 * CONV_WIDTH), F32),
        ],
        compiler_params=_cparams(("arbitrary", "arbitrary")),
        name="mixer_in_proj",
    )(x, mod3, norm_g, w_in_r, w_wi_t)


def _key_to_f32(key):
    return lax.bitcast_convert_type(jnp.where(key >= 0, key, key ^ 0x7FFFFFFF), F32)


def _att_kernel(q_ref, k_ref, v_ref, qi_ref, ki_ref, wi_ref, o_ref,
                idx_scr, xb_scr, lg_scr, m_scr, acc_scr, j_scr, thr_scr, ngt_scr, *, tq, kb, topk, n_tie_iter, nq):
    j = pl.program_id(1)
    nkb = (j * tq + tq + kb - 1) // kb
    heads = ATT_HEADS
    rows = heads * tq
    half = kb // 2

    qcol = lax.broadcasted_iota(jnp.int32, (1, tq), 1)
    limit = j * tq + ((qcol // CHUNK) + 1) * CHUNK
    key0 = lax.broadcasted_iota(jnp.int32, (kb, tq), 0)
    wt = wi_ref[...]

    def idx_body(b, carry):
        kib = ki_ref[pl.ds(pl.multiple_of(b * kb, kb), kb), :]
        acc = jnp.zeros((kb, tq), F32)
        for hd in range(IDX_HEADS):
            acc = acc + jnp.maximum(_nt_dot(kib, qi_ref[hd]), 0.0) * wt[hd:hd + 1, :]
        score = jnp.where(key0 + b * kb < limit, acc, -jnp.inf)
        idx_scr[b] = score
        xb_scr[b] = score.astype(BF16)
        return carry

    lax.fori_loop(0, nkb, idx_body, 0)

    def over_blocks(nb, body, init):
        if isinstance(nb, int):
            acc = init
            for b in range(nb):
                acc = body(b, acc)
            return acc
        return lax.fori_loop(0, nb, body, init)

    def reduce_keys(fn, init, combine, nb=nkb):
        def body(b, acc):
            return combine(acc, fn(idx_scr[b], b).reshape(kb // 8, 8, tq))
        return over_blocks(nb, body, jnp.full((8, tq), init, F32))

    def count(pred, nb=nkb):
        acc = reduce_keys(lambda blk, b: jnp.where(pred(blk, b), 1.0, 0.0), 0.0,
                          lambda a, m: a + jnp.sum(m, axis=0), nb)
        return jnp.sum(acc, axis=0, keepdims=True)

    one_b, zero_b = jnp.ones((16, tq), BF16), jnp.zeros((16, tq), BF16)

    def count16(cand_b, nb):
        def body(b, acc):
            blk = xb_scr[b]
            for i in range(kb // 16):
                acc = acc + jnp.where(blk[16 * i:16 * (i + 1)] >= cand_b, one_b, zero_b)
            return acc
        return jnp.sum(over_blocks(nb, body, zero_b).astype(F32), axis=0, keepdims=True)

    def search(nb):
        def coarse_body(_, carry):
            lo, hi = carry
            mid = (lo + hi + 1) >> 1
            pattern = jnp.where(mid >= 0, mid, mid ^ 0x7FFF) << 16
            cand = lax.bitcast_convert_type(pattern, F32).astype(BF16)
            ok = count16(jnp.broadcast_to(cand, (16, tq)), nb) >= topk
            return jnp.where(ok, mid, lo), jnp.where(ok, hi, mid - 1)

        c16, _ = lax.fori_loop(0, 16, coarse_body, (jnp.full((1, tq), KEY_NEG_INF >> 16, jnp.int32),
                                                    jnp.full((1, tq), KEY_POS_INF >> 16, jnp.int32)))
        key_t16 = jnp.where(c16 >= 0, c16 << 16, (c16 << 16) | 0xFFFF)

        def fine_body(_, carry):
            lo, hi = carry
            mid = (lo | hi) - ((lo ^ hi) >> 1)
            cand = _key_to_f32(mid)
            ok = count(lambda blk, b: blk >= cand, nb) >= topk
            return jnp.where(ok, mid, lo), jnp.where(ok, hi, mid - 1)

        lo, _ = lax.fori_loop(0, 17, fine_body, (jnp.maximum(key_t16 - 32770, KEY_NEG_INF),
                                                 jnp.minimum(key_t16 + 65536, KEY_POS_INF)))
        thr = _key_to_f32(lo)
        return thr, count(lambda blk, b: blk > thr, nb)

    for jj in range(nq):
        @pl.when(j == jj)
        def _(jj=jj):
            thr_s, n_gt_s = search((jj * tq + tq + kb - 1) // kb)
            thr_scr[...] = jnp.broadcast_to(thr_s, (8, tq))
            ngt_scr[...] = jnp.broadcast_to(n_gt_s, (8, tq))

    thr = thr_scr[0:1, :]
    n_gt = ngt_scr[0:1, :]

    def refine_cond(carry):
        return jnp.max(jnp.where(carry[1] >= topk, 1, 0)) > 0

    def refine_body(carry):
        cur, cnt = carry
        above = reduce_keys(lambda blk, b: jnp.where(blk > cur, blk, jnp.inf), jnp.inf,
                            lambda a, m: jnp.minimum(a, jnp.min(m, axis=0)))
        cur = jnp.where(cnt >= topk, jnp.min(above, axis=0, keepdims=True), cur)
        return cur, count(lambda blk, b: blk > cur)

    thr, n_gt = lax.while_loop(refine_cond, refine_body, (thr, n_gt))
    need = topk - n_gt
    n_eq = count(lambda blk, b: blk == thr)
    has_tie = jnp.logical_and(n_eq > need, thr > -jnp.inf)
    j_max = nkb * kb - 1
    j_scr[...] = jnp.full((8, tq), j_max, jnp.int32)

    @pl.when(jnp.max(jnp.where(has_tie, 1, 0)) > 0)
    def _():
        def tie_body(_, carry):
            tlo, thi = carry
            mid = (tlo + thi) >> 1
            ok = count(lambda blk, b: jnp.logical_and(blk == thr, key0 + b * kb <= mid)) >= need
            return jnp.where(ok, tlo, mid + 1), jnp.where(ok, mid, thi)
        tlo, _ = lax.fori_loop(0, n_tie_iter, tie_body,
                               (jnp.zeros((1, tq), jnp.int32), jnp.full((1, tq), j_max, jnp.int32)))
        j_scr[...] = jnp.broadcast_to(jnp.where(has_tie, tlo, j_max), (8, tq))

    j_sel = jnp.where(thr > -jnp.inf, j_scr[0:1, :], -1)

    qall = q_ref[...].reshape(rows, ATT_HEAD_DIM)
    m_scr[...] = jnp.full((rows, half), NEG_BIG, F32)

    def logits_body(b, carry):
        kblk = k_ref[pl.ds(pl.multiple_of(b * kb, kb), kb), :]
        blk = idx_scr[b]
        sel = jnp.logical_or(blk > thr, jnp.logical_and(blk == thr, key0 + b * kb <= j_sel))
        bias = jnp.where(sel, 0.0, NEG_BIG).T
        lg = _nt_dot(qall, kblk).reshape(heads, tq, kb) + bias[None]
        lg_scr[b] = lg
        lg2 = lg.reshape(rows, kb)
        m_scr[...] = jnp.maximum(m_scr[...], jnp.maximum(lg2[:, :half], lg2[:, half:]))
        return carry

    lax.fori_loop(0, nkb, logits_body, 0)
    m_scr[...] = jnp.broadcast_to(jnp.max(m_scr[...], axis=1, keepdims=True), (rows, half))

    acc_scr[...] = jnp.zeros((rows, 2 * ATT_HEAD_DIM), F32)

    def pv_body(b, carry):
        vblk = v_ref[pl.ds(pl.multiple_of(b * kb, kb), kb), :]
        lg2 = lg_scr[b].reshape(rows, kb)
        mrow = m_scr[...]
        p = jnp.concatenate([jnp.exp2(lg2[:, :half] - mrow), jnp.exp2(lg2[:, half:] - mrow)], axis=1).astype(BF16)
        acc_scr[...] += jnp.dot(p, vblk, preferred_element_type=F32)
        return carry

    lax.fori_loop(0, nkb, pv_body, 0)
    acc = acc_scr[...]
    out = acc[:, :ATT_HEAD_DIM] / acc[:, ATT_HEAD_DIM:ATT_HEAD_DIM + 1]
    for hd in range(heads):
        o_ref[:, hd * ATT_HEAD_DIM:(hd + 1) * ATT_HEAD_DIM] = out[hd * tq:(hd + 1) * tq]


def _att_call(q, k, v, qi, ki, wi, *, tq, kb):
    bsz, _, seq, _ = q.shape
    topk = min(TOPK_MAX, seq // TOPK_FRACTION)
    assert tq % CHUNK == 0 and tq >= topk and seq % tq == 0 and seq % kb == 0 and kb % 256 == 0
    nq = seq // tq
    nkb_max = seq // kb
    assert (kb // 16) * nkb_max <= 256
    rows = ATT_HEADS * tq
    kern = functools.partial(_att_kernel, tq=tq, kb=kb, topk=topk, n_tie_iter=max(1, math.ceil(math.log2(seq))),
                             nq=nq)
    return pl.pallas_call(
        kern,
        grid=(bsz, nq),
        in_specs=[
            pl.BlockSpec((None, ATT_HEADS, tq, ATT_HEAD_DIM), lambda b, j: (b, 0, j, 0)),
            pl.BlockSpec((None, seq, ATT_HEAD_DIM), lambda b, j: (b, 0, 0)),
            pl.BlockSpec((None, seq, 2 * ATT_HEAD_DIM), lambda b, j: (b, 0, 0)),
            pl.BlockSpec((None, IDX_HEADS, tq, IDX_DIM), lambda b, j: (b, 0, j, 0)),
            pl.BlockSpec((None, seq, IDX_DIM), lambda b, j: (b, 0, 0)),
            pl.BlockSpec((None, IDX_HEADS, tq), lambda b, j: (b, 0, j)),
        ],
        out_specs=pl.BlockSpec((None, tq, ATT_WIDTH), lambda b, j: (b, j, 0)),
        out_shape=jax.ShapeDtypeStruct((bsz, seq, ATT_WIDTH), F32),
        scratch_shapes=[
            pltpu.VMEM((nkb_max, kb, tq), F32),
            pltpu.VMEM((nkb_max, kb, tq), BF16),
            pltpu.VMEM((nkb_max, ATT_HEADS, tq, kb), F32),
            pltpu.VMEM((rows, kb // 2), F32),
            pltpu.VMEM((rows, 2 * ATT_HEAD_DIM), F32),
            pltpu.VMEM((8, tq), jnp.int32),
            pltpu.VMEM((8, tq), F32),
            pltpu.VMEM((8, tq), F32),
        ],
        compiler_params=_cparams(("arbitrary", "arbitrary")),
        name="dsa_attention",
    )(q, k, v, qi, ki, wi)


def _ssm_kernel(u_ref, kbd_ref, bm_ref, cm_ref, a8_ref, d_ref, wglu_ref, o_ref, st_scr, y_scr, u_scr, *, seq):
    nc = seq // SSM_SUB
    half = SSM_LANES // 2
    u = u_ref[...]
    sub = lax.broadcasted_iota(jnp.int32, (seq, 1), 0) % SSM_SUB
    y = u * d_ref[...] + jnp.dot(u.astype(BF16), kbd_ref[0], preferred_element_type=F32)
    for tau in range(1, SSM_SUB):
        shifted = jnp.where(sub >= tau, pltpu.roll(u, tau, 0), 0.0).astype(BF16)
        y = y + jnp.dot(shifted, kbd_ref[tau], preferred_element_type=F32)
    for hf in range(2):
        y_scr[hf] = y[:, hf * 128:(hf + 1) * 128]
        u_scr[hf] = u[:, hf * 128:(hf + 1) * 128]

    e = jnp.zeros((nc, SSM_LANES), F32)
    for s in range(SSM_SUB):
        rows = pl.ds(s, nc, stride=SSM_SUB)
        us = jnp.concatenate([u_scr[0, rows, :], u_scr[1, rows, :]], axis=1).astype(BF16)
        e = e + jnp.dot(us, bm_ref[s], preferred_element_type=F32)
    st_scr[...] = e

    a8r = a8_ref[:, :half]
    a8i = a8_ref[:, half:]

    def scan_body(c, carry):
        xr, xi = carry
        inc = st_scr[pl.ds(c, 1), :]
        st_scr[pl.ds(c, 1), :] = jnp.concatenate([xr, xi], axis=1)
        return (a8r * xr - a8i * xi + inc[:, :half], a8r * xi + a8i * xr + inc[:, half:])

    lax.fori_loop(0, nc, scan_body, (jnp.zeros((1, half), F32), jnp.zeros((1, half), F32)))

    x_in = st_scr[...].astype(BF16)
    for i in range(SSM_SUB):
        rows = pl.ds(i, nc, stride=SSM_SUB)
        yi = jnp.dot(x_in, cm_ref[i], preferred_element_type=F32)
        for hf in range(2):
            y_scr[hf, rows, :] = y_scr[hf, rows, :] + yi[:, hf * 128:(hf + 1) * 128]

    act = jax.nn.gelu(jnp.concatenate([y_scr[0], y_scr[1]], axis=1)).astype(BF16)
    ag = jnp.dot(act, wglu_ref[...], preferred_element_type=F32)
    o_ref[...] = ag[:, :SSM_WIDTH] * jax.nn.sigmoid(ag[:, SSM_WIDTH:])


def _ssm_call(u, kbd, bm, cm, a8, d, w_glu):
    bsz, seq, _ = u.shape
    assert seq % SSM_SUB == 0
    return pl.pallas_call(
        functools.partial(_ssm_kernel, seq=seq),
        grid=(bsz,),
        in_specs=[
            pl.BlockSpec((None, seq, SSM_WIDTH), lambda b: (b, 0, 0)),
            _resident((SSM_SUB, SSM_WIDTH, SSM_WIDTH)),
            _resident((SSM_SUB, SSM_WIDTH, SSM_LANES)),
            _resident((SSM_SUB, SSM_LANES, SSM_WIDTH)),
            _resident((1, SSM_LANES)),
            _resident((1, SSM_WIDTH)),
            _resident((SSM_WIDTH, 2 * SSM_WIDTH)),
        ],
        out_specs=pl.BlockSpec((None, seq, SSM_WIDTH), lambda b: (b, 0, 0)),
        out_shape=jax.ShapeDtypeStruct((bsz, seq, SSM_WIDTH), F32),
        scratch_shapes=[pltpu.VMEM((seq // SSM_SUB, SSM_LANES), F32), pltpu.VMEM((2, seq, 128), F32),
                        pltpu.VMEM((2, seq, 128), F32)],
        compiler_params=_cparams(("arbitrary",)),
        name="s5_ssm",
    )(u, kbd, bm, cm, a8, d, w_glu)


def _ssm_params(a_re, a_im, log_dt, b_re, b_im, c_re, c_im, d_skip):
    dt = jnp.exp(log_dt)[:, None]
    mag = jnp.exp(dt * a_re)
    abr, abi = mag * jnp.cos(dt * a_im), mag * jnp.sin(dt * a_im)
    den = a_re * a_re + a_im * a_im
    nr, ni = abr - 1.0, abi
    qr, qi = (nr * a_re + ni * a_im) / den, (ni * a_re - nr * a_im) / den
    bbr = qr[..., None] * b_re - qi[..., None] * b_im
    bbi = qr[..., None] * b_im + qi[..., None] * b_re
    pr, pi = [jnp.ones_like(abr)], [jnp.zeros_like(abr)]
    for _ in range(SSM_SUB):
        pr, pi = pr + [pr[-1] * abr - pi[-1] * abi], pi + [pr[-1] * abi + pi[-1] * abr]
    pwr, pwi = jnp.stack(pr), jnp.stack(pi)
    eye = jnp.eye(SSM_GROUPS, dtype=F32)
    abr_t = pwr[:SSM_SUB, :, :, None] * bbr[None] - pwi[:SSM_SUB, :, :, None] * bbi[None]
    abi_t = pwr[:SSM_SUB, :, :, None] * bbi[None] + pwi[:SSM_SUB, :, :, None] * bbr[None]
    kt = (jnp.einsum('gop,tgpc->tgco', c_re, abr_t, precision=HI)
          - jnp.einsum('gop,tgpc->tgco', c_im, abi_t, precision=HI))
    kbd = jnp.einsum('tgco,gh->tgcho', kt, eye).reshape(SSM_SUB, SSM_WIDTH, SSM_WIDTH)
    rev_r = jnp.stack([abr_t[SSM_SUB - 1 - s] for s in range(SSM_SUB)])
    rev_i = jnp.stack([abi_t[SSM_SUB - 1 - s] for s in range(SSM_SUB)])
    bm_r = jnp.einsum('sgpc,gh->sgchp', rev_r, eye).reshape(SSM_SUB, SSM_WIDTH, SSM_GROUPS * SSM_STATE)
    bm_i = jnp.einsum('sgpc,gh->sgchp', rev_i, eye).reshape(SSM_SUB, SSM_WIDTH, SSM_GROUPS * SSM_STATE)
    bm = jnp.concatenate([bm_r, bm_i], axis=-1)
    ar1, ai1 = pwr[1:, :, None, :], pwi[1:, :, None, :]
    mr = c_re[None] * ar1 - c_im[None] * ai1
    mi = c_re[None] * ai1 + c_im[None] * ar1
    cm_r = jnp.einsum('igop,gh->igpho', mr, eye).reshape(SSM_SUB, SSM_GROUPS * SSM_STATE, SSM_WIDTH)
    cm_i = jnp.einsum('igop,gh->igpho', -mi, eye).reshape(SSM_SUB, SSM_GROUPS * SSM_STATE, SSM_WIDTH)
    cm = jnp.concatenate([cm_r, cm_i], axis=1)
    a8 = jnp.concatenate([pwr[SSM_SUB].reshape(1, -1), pwi[SSM_SUB].reshape(1, -1)], axis=1)
    return kbd.astype(BF16), bm.astype(BF16), cm.astype(BF16), a8, d_skip.reshape(1, SSM_WIDTH)


def _conv_kernel(cur_ref, halo_ref, wdw_ref, bdw_ref, lng_ref, lnb_ref, wpw_ref, o_ref, pad_scr, *, rb):
    t = pl.program_id(1)

    def glu(blk):
        return blk[:, :CONV_WIDTH] * jax.nn.sigmoid(blk[:, CONV_WIDTH:])

    pad_scr[0:CONV_HALO, :] = glu(halo_ref[...]) * (t > 0).astype(F32)
    pad_scr[CONV_HALO:CONV_HALO + rb, :] = glu(cur_ref[...])
    pad_scr[CONV_HALO + rb:, :] = jnp.zeros((8, CONV_WIDTH), F32)
    y = jnp.zeros((rb, CONV_WIDTH), F32) + bdw_ref[...]
    for phase in range(8):
        part = None
        for tap in range(CONV_KSIZE):
            off = CONV_HALO - (CONV_KSIZE - 1) + tap
            if off % 8 != phase:
                continue
            term = pad_scr[off - phase:off - phase + rb + 8, :] * wdw_ref[tap:tap + 1, :]
            part = term if part is None else part + term
        y = y + part[phase:phase + rb, :]
    mu = jnp.mean(y, axis=-1, keepdims=True)
    var = jnp.mean(jnp.square(y - mu), axis=-1, keepdims=True)
    z = (y - mu) * lax.rsqrt(var + NORM_EPS) * lng_ref[...] + lnb_ref[...]
    z = (z * jax.nn.sigmoid(z)).astype(BF16)
    o_ref[...] = jnp.dot(z, wpw_ref[...], preferred_element_type=F32)


def _conv_call(uc, w_dw, b_dw, ln_g, ln_b, w_pw, *, rb):
    bsz, seq, _ = uc.shape
    assert seq % rb == 0 and rb % CONV_HALO == 0
    per = rb // CONV_HALO
    return pl.pallas_call(
        functools.partial(_conv_kernel, rb=rb),
        grid=(bsz, seq // rb),
        in_specs=[
            pl.BlockSpec((None, rb, 2 * CONV_WIDTH), lambda b, t: (b, t, 0)),
            pl.BlockSpec((None, CONV_HALO, 2 * CONV_WIDTH), lambda b, t: (b, jnp.maximum(t * per - 1, 0), 0)),
            _resident((CONV_KSIZE, CONV_WIDTH)),
            _resident((1, CONV_WIDTH)),
            _resident((1, CONV_WIDTH)),
            _resident((1, CONV_WIDTH)),
            _resident((CONV_WIDTH, CONV_WIDTH)),
        ],
        out_specs=pl.BlockSpec((None, rb, CONV_WIDTH), lambda b, t: (b, t, 0)),
        out_shape=jax.ShapeDtypeStruct((bsz, seq, CONV_WIDTH), F32),
        scratch_shapes=[pltpu.VMEM((rb + CONV_HALO + 8, CONV_WIDTH), F32)],
        compiler_params=_cparams(("arbitrary", "arbitrary")),
        name="conformer_conv",
    )(uc, uc, w_dw, b_dw, ln_g, ln_b, w_pw)


def _reorder_w_in(w_in):
    splits = [0]
    for c in IN_COLS:
        splits.append(splits[-1] + c)
    q, k, v, qi, ki, wi, us, uc = [w_in[:, splits[i]:splits[i + 1]] for i in range(len(IN_COLS))]
    pad = jnp.zeros((w_in.shape[0], PROJ_SSM0 - PROJ_KW0 - IDX_DIM - IDX_HEADS), w_in.dtype)
    return jnp.concatenate([q, k, v, qi, ki, wi, pad, us, uc], axis=1).astype(BF16)


def kernel(x, c, mod_w, mod_b, ffn1_norm, ffn1_w_gu, ffn1_w_down, mix_norm, w_in, w_out, att_out_norm, ssm_out_norm, conv_out_norm, ssm_a_re, ssm_a_im, ssm_log_dt, ssm_b_re, ssm_b_im, ssm_c_re, ssm_c_im, ssm_d, ssm_w_glu, conv_w_dw, conv_b_dw, conv_ln_g, conv_ln_b, conv_w_pw, ffn2_norm, ffn2_w_gu, ffn2_w_down, final_norm):
    bsz, seq, _ = x.shape
    depth = mod_w.shape[0]
    tm = min(512, seq)
    tq = min(256, seq)
    mod = _mod_call(c, mod_w, mod_b).reshape(depth, bsz, N_MOD, D_MODEL)
    row = lambda v: v.reshape(1, -1)
    fin = row(final_norm)
    for l in range(depth):
        mod3 = mod[l]
        x = _ffn_call(x, mod3, row(ffn1_norm[l]), ffn1_w_gu[l].astype(BF16), ffn1_w_down[l].astype(BF16), fin,
                      row0=0, final=False, tm=tm)
        wi0 = sum(IN_COLS[:5])
        w_wi_t = w_in[l][:, wi0:wi0 + IDX_HEADS].T.astype(BF16)
        q, k, v, qi, ki, wi, us, uc = _proj_call(x, mod3, row(mix_norm[l]), _reorder_w_in(w_in[l]), w_wi_t, tm=tm)
        y_att = _att_call(q, k, v, qi, ki, wi, tq=tq, kb=256)
        y_ssm = _ssm_call(us, *_ssm_params(ssm_a_re[l], ssm_a_im[l], ssm_log_dt[l], ssm_b_re[l], ssm_b_im[l],
                                           ssm_c_re[l], ssm_c_im[l], ssm_d[l]), ssm_w_glu[l].astype(BF16))
        y_conv = _conv_call(uc, conv_w_dw[l], row(conv_b_dw[l]), row(conv_ln_g[l]), row(conv_ln_b[l]),
                            conv_w_pw[l].astype(BF16), rb=min(256, seq))
        x = _mix_ffn_call(x, mod3, y_att, y_ssm, y_conv, row(att_out_norm[l]), row(ssm_out_norm[l]),
                          row(conv_out_norm[l]), w_out[l].astype(BF16), row(ffn2_norm[l]),
                          ffn2_w_gu[l].astype(BF16), ffn2_w_down[l].astype(BF16), fin,
                          final=(l == depth - 1), tm=tm)
    return x
```

```python
import functools
import math

import jax
import jax.numpy as jnp
from jax import lax
from jax.experimental import pallas as pl
from jax.experimental.pallas import tpu as pltpu

F32 = jnp.float32
BF16 = jnp.bfloat16

D_MODEL = 1024
CHUNK = 64
ATT_HEADS = 8
ATT_HEAD_DIM = 64
ATT_WIDTH = ATT_HEADS * ATT_HEAD_DIM
IDX_HEADS = 8
IDX_DIM = 32
TOPK_MAX = 256
TOPK_FRACTION = 4
SSM_WIDTH = 256
SSM_GROUP = 16
SSM_GROUPS = 16
SSM_STATE = 64
SSM_LANES = 2 * SSM_GROUPS * SSM_STATE
SSM_SUB = 8
CONV_WIDTH = 256
CONV_KSIZE = 31
CONV_HALO = 32
D_FF = 2816
N_MOD = 9
NORM_EPS = 1e-6
IN_COLS = (ATT_WIDTH, ATT_HEAD_DIM, ATT_HEAD_DIM, IDX_HEADS * IDX_DIM, IDX_DIM, IDX_HEADS, SSM_WIDTH, 2 * CONV_WIDTH)

VMEM_LIMIT = 56 * 1024 * 1024
NEG_BIG = -1e30
LOG2_E = 1.4426950408889634
KEY_NEG_INF = -2139095041
KEY_POS_INF = 2139095040
HI = lax.Precision.HIGHEST


def _cparams(sem):
    return pltpu.CompilerParams(dimension_semantics=sem, vmem_limit_bytes=VMEM_LIMIT)


def _resident(shape):
    nd = len(shape)
    return pl.BlockSpec(shape, lambda *_: (0,) * nd, pipeline_mode=pl.Buffered(1))


def _nt_dot(a, b):
    return lax.dot_general(a, b, (((1,), (1,)), ((), ())), preferred_element_type=F32)


def _modulated_norm(x, gain, shift, scale):
    y = x * lax.rsqrt(jnp.mean(x * x, axis=-1, keepdims=True) + NORM_EPS) * gain
    return y * (1.0 + scale) + shift


def _mod_kernel(c_ref, w_ref, b_ref, o_ref):
    c = c_ref[...]
    cond = c * jax.nn.sigmoid(c)
    o_ref[...] = jnp.dot(cond, w_ref[...], preferred_element_type=F32, precision=HI) + b_ref[...]


def _mod_call(c, mod_w, mod_b):
    depth = mod_w.shape[0]
    bsz = c.shape[0]
    return pl.pallas_call(
        _mod_kernel,
        grid=(depth, N_MOD),
        in_specs=[
            pl.BlockSpec((bsz, D_MODEL), lambda l, j: (0, 0)),
            pl.BlockSpec((None, D_MODEL, D_MODEL), lambda l, j: (l, 0, j)),
            pl.BlockSpec((None, 1, D_MODEL), lambda l, j: (l, 0, j)),
        ],
        out_specs=pl.BlockSpec((None, bsz, D_MODEL), lambda l, j: (l, 0, j)),
        out_shape=jax.ShapeDtypeStruct((depth, bsz, N_MOD * D_MODEL), F32),
        compiler_params=_cparams(("arbitrary", "arbitrary")),
        name="adaln_mod",
    )(c, mod_w, mod_b.reshape(depth, 1, N_MOD * D_MODEL))


FFN_CHUNK = 256


def _rms_gain(y, gain):
    return (y * lax.rsqrt(jnp.mean(y * y, axis=-1, keepdims=True) + NORM_EPS) * gain).astype(BF16)


def _mix_ffn_kernel(x_ref, mod_ref, ya_ref, ys_ref, yc_ref, ga_ref, gs_ref, gc_ref, wo_ref,
                    ng_ref, wgu_ref, wd_ref, fn_ref, o_ref, *, final):
    mix = jnp.dot(_rms_gain(ya_ref[...], ga_ref[...]), wo_ref[0:ATT_WIDTH, :], preferred_element_type=F32)
    mix = mix + jnp.dot(_rms_gain(ys_ref[...], gs_ref[...]), wo_ref[ATT_WIDTH:ATT_WIDTH + SSM_WIDTH, :],
                        preferred_element_type=F32)
    mix = mix + jnp.dot(_rms_gain(yc_ref[...], gc_ref[...]), wo_ref[ATT_WIDTH + SSM_WIDTH:, :],
                        preferred_element_type=F32)
    x = x_ref[...] + mod_ref[5:6, :] * mix
    o_ref[...] = _ffn_math(x, mod_ref, ng_ref, wgu_ref, wd_ref, fn_ref, row0=6, final=final)


def _ffn_math(x, mod_ref, ng_ref, wgu_ref, wd_ref, fn_ref, *, row0, final):
    h = _modulated_norm(x, ng_ref[...], mod_ref[row0:row0 + 1, :], mod_ref[row0 + 1:row0 + 2, :]).astype(BF16)
    acc = jnp.zeros(x.shape, F32)
    for c0 in range(0, D_FF, FFN_CHUNK):
        gate = jnp.dot(h, wgu_ref[:, c0:c0 + FFN_CHUNK], preferred_element_type=F32)
        up = jnp.dot(h, wgu_ref[:, D_FF + c0:D_FF + c0 + FFN_CHUNK], preferred_element_type=F32)
        a = (gate * jax.nn.sigmoid(gate) * up).astype(BF16)
        acc = acc + jnp.dot(a, wd_ref[c0:c0 + FFN_CHUNK, :], preferred_element_type=F32)
    out = x + (0.5 * mod_ref[row0 + 2:row0 + 3, :]) * acc
    if final:
        out = out * lax.rsqrt(jnp.mean(out * out, axis=-1, keepdims=True) + NORM_EPS) * fn_ref[...]
    return out


def _mix_ffn_call(x, mod3, y_att, y_ssm, y_conv, g_att, g_ssm, g_conv, w_out, norm_g, w_gu, w_down, final_g,
                  *, final, tm):
    bsz, seq, _ = x.shape
    tok = lambda width: pl.BlockSpec((None, tm, width), lambda b, t: (b, t, 0))
    return pl.pallas_call(
        functools.partial(_mix_ffn_kernel, final=final),
        grid=(bsz, seq // tm),
        in_specs=[
            tok(D_MODEL),
            pl.BlockSpec((None, N_MOD, D_MODEL), lambda b, t: (b, 0, 0)),
            tok(ATT_WIDTH), tok(SSM_WIDTH), tok(CONV_WIDTH),
            _resident((1, ATT_WIDTH)), _resident((1, SSM_WIDTH)), _resident((1, CONV_WIDTH)),
            _resident((D_MODEL, D_MODEL)),
            _resident((1, D_MODEL)),
            _resident((D_MODEL, 2 * D_FF)),
            _resident((D_FF, D_MODEL)),
            _resident((1, D_MODEL)),
        ],
        out_specs=tok(D_MODEL),
        out_shape=jax.ShapeDtypeStruct(x.shape, F32),
        compiler_params=_cparams(("arbitrary", "arbitrary")),
        name="mix_ffn_final" if final else "mix_ffn",
    )(x, mod3, y_att, y_ssm, y_conv, g_att, g_ssm, g_conv, w_out, norm_g, w_gu, w_down, final_g)


PROJ_Q0, PROJ_KV0, PROJ_QI0, PROJ_KW0, PROJ_SSM0, PROJ_CONV0, PROJ_END = 0, 512, 640, 896, 1024, 1280, 1792


def _ffn_proj_kernel(x_ref, mod_ref, fng_ref, wgu_ref, wd_ref, fn_ref, ng_ref, w_ref, wit_ref,
                     xo_ref, q_ref, k_ref, v_ref, qi_ref, ki_ref, wi_ref, us_ref, uc_ref):
    x = _ffn_math(x_ref[...], mod_ref, fng_ref, wgu_ref, wd_ref, fn_ref, row0=0, final=False)
    xo_ref[...] = x
    h = _modulated_norm(x, ng_ref[...], mod_ref[3:4, :], mod_ref[4:5, :]).astype(BF16)
    q = jnp.dot(h, w_ref[:, PROJ_Q0:PROJ_KV0], preferred_element_type=F32) * (ATT_HEAD_DIM ** -0.5 * LOG2_E)
    for hd in range(ATT_HEADS):
        q_ref[hd] = q[:, hd * ATT_HEAD_DIM:(hd + 1) * ATT_HEAD_DIM].astype(BF16)
    kv = jnp.dot(h, w_ref[:, PROJ_KV0:PROJ_QI0], preferred_element_type=F32)
    k_ref[...] = kv[:, :ATT_HEAD_DIM].astype(BF16)
    lane = lax.broadcasted_iota(jnp.int32, kv.shape, 1)
    ones_col = jnp.where(lane == ATT_HEAD_DIM, 1.0, 0.0)
    v_ref[...] = jnp.where(lane < ATT_HEAD_DIM, pltpu.roll(kv, ATT_HEAD_DIM, 1), ones_col).astype(BF16)
    qi = jnp.dot(h, w_ref[:, PROJ_QI0:PROJ_KW0], preferred_element_type=F32)
    for hd in range(IDX_HEADS):
        qi_ref[hd] = qi[:, hd * IDX_DIM:(hd + 1) * IDX_DIM].astype(BF16)
    kw = jnp.dot(h, w_ref[:, PROJ_KW0:PROJ_SSM0], preferred_element_type=F32)
    ki_ref[...] = kw[:, :IDX_DIM].astype(BF16)
    wi_ref[...] = _nt_dot(wit_ref[...], h) * (IDX_DIM ** -0.5 * IDX_HEADS ** -0.5)
    us_ref[...] = jnp.dot(h, w_ref[:, PROJ_SSM0:PROJ_CONV0], preferred_element_type=F32)
    uc_ref[...] = jnp.dot(h, w_ref[:, PROJ_CONV0:PROJ_END], preferred_element_type=F32)


def _ffn_proj_call(x, mod3, ffn_norm_g, w_gu, w_down, final_g, norm_g, w_in_r, w_wi_t, *, tm):
    bsz, seq, _ = x.shape
    nt = seq // tm
    tok = lambda width: pl.BlockSpec((None, tm, width), lambda b, t: (b, t, 0))
    head = lambda n, width: pl.BlockSpec((None, n, tm, width), lambda b, t: (b, 0, t, 0))
    return pl.pallas_call(
        _ffn_proj_kernel,
        grid=(bsz, nt),
        in_specs=[
            tok(D_MODEL),
            pl.BlockSpec((None, N_MOD, D_MODEL), lambda b, t: (b, 0, 0)),
            _resident((1, D_MODEL)),
            _resident((D_MODEL, 2 * D_FF)),
            _resident((D_FF, D_MODEL)),
            _resident((1, D_MODEL)),
            _resident((1, D_MODEL)),
            _resident((D_MODEL, PROJ_END)),
            _resident((IDX_HEADS, D_MODEL)),
        ],
        out_specs=[
            tok(D_MODEL),
            head(ATT_HEADS, ATT_HEAD_DIM), tok(ATT_HEAD_DIM), tok(2 * ATT_HEAD_DIM),
            head(IDX_HEADS, IDX_DIM), tok(IDX_DIM),
            pl.BlockSpec((None, IDX_HEADS, tm), lambda b, t: (b, 0, t)),
            tok(SSM_WIDTH), tok(2 * CONV_WIDTH),
        ],
        out_shape=[
            jax.ShapeDtypeStruct(x.shape, F32),
            jax.ShapeDtypeStruct((bsz, ATT_HEADS, seq, ATT_HEAD_DIM), BF16),
            jax.ShapeDtypeStruct((bsz, seq, ATT_HEAD_DIM), BF16),
            jax.ShapeDtypeStruct((bsz, seq, 2 * ATT_HEAD_DIM), BF16),
            jax.ShapeDtypeStruct((bsz, IDX_HEADS, seq, IDX_DIM), BF16),
            jax.ShapeDtypeStruct((bsz, seq, IDX_DIM), BF16),
            jax.ShapeDtypeStruct((bsz, IDX_HEADS, seq), F32),
            jax.ShapeDtypeStruct((bsz, seq, SSM_WIDTH), F32),
            jax.ShapeDtypeStruct((bsz, seq, 2 * CONV_WIDTH), F32),
        ],
        compiler_params=_cparams(("arbitrary", "arbitrary")),
        name="ffn_in_proj",
    )(x, mod3, ffn_norm_g, w_gu, w_down, final_g, norm_g, w_in_r, w_wi_t)


def _key_to_f32(key):
    return lax.bitcast_convert_type(jnp.where(key >= 0, key, key ^ 0x7FFFFFFF), F32)


def _att_kernel(q_ref, k_ref, v_ref, qi_ref, ki_ref, wi_ref, o_ref,
                idx_scr, xb_scr, lg_scr, m_scr, acc_scr, j_scr, thr_scr, cnt_scr, *, tq, kb, topk, n_tie_iter, nq):
    j = pl.program_id(1)
    nkb = (j * tq + tq + kb - 1) // kb
    heads = ATT_HEADS
    rows = heads * tq
    half = kb // 2

    qcol = lax.broadcasted_iota(jnp.int32, (1, tq), 1)
    limit = j * tq + ((qcol // CHUNK) + 1) * CHUNK
    key0 = lax.broadcasted_iota(jnp.int32, (kb, tq), 0)
    wt = wi_ref[...]

    def for_key_blocks(body):
        n4 = nkb // 4

        def quad(i, carry):
            for u in range(4):
                body(i * 4 + u)
            return carry

        lax.fori_loop(0, n4, quad, 0)
        rem = nkb - n4 * 4

        @pl.when((rem & 2) != 0)
        def _():
            body(n4 * 4)
            body(n4 * 4 + 1)

        @pl.when((rem & 1) != 0)
        def _():
            body(n4 * 4 + (rem & 2))

    def idx_body(b):
        kib = ki_ref[pl.ds(pl.multiple_of(b * kb, kb), kb), :]
        acc = jnp.zeros((kb, tq), F32)
        for hd in range(IDX_HEADS):
            acc = acc + jnp.maximum(_nt_dot(kib, qi_ref[hd]), 0.0) * wt[hd:hd + 1, :]
        score = jnp.where(key0 + b * kb < limit, acc, -jnp.inf)
        idx_scr[b] = score
        xb_scr[b] = score.astype(BF16)

    for_key_blocks(idx_body)

    def over_blocks(nb, body, init):
        if isinstance(nb, int):
            acc = init
            for b in range(nb):
                acc = body(b, acc)
            return acc
        return lax.fori_loop(0, nb, body, init)

    def reduce_keys(fn, init, combine, nb=nkb):
        def body(b, acc):
            return combine(acc, fn(idx_scr[b], b).reshape(kb // 8, 8, tq))
        return over_blocks(nb, body, jnp.full((8, tq), init, F32))

    def count(pred, nb=nkb):
        acc = reduce_keys(lambda blk, b: jnp.where(pred(blk, b), 1.0, 0.0), 0.0,
                          lambda a, m: a + jnp.sum(m, axis=0), nb)
        return jnp.sum(acc, axis=0, keepdims=True)

    one_b, zero_b = jnp.ones((16, tq), BF16), jnp.zeros((16, tq), BF16)

    def count16(cand_b, nb):
        def body(b, acc):
            blk = xb_scr[b]
            for i in range(kb // 16):
                acc = acc + jnp.where(blk[16 * i:16 * (i + 1)] >= cand_b, one_b, zero_b)
            return acc
        return jnp.sum(over_blocks(nb, body, zero_b).astype(F32), axis=0, keepdims=True)

    def search(nb):
        def coarse_body(_, carry):
            lo, hi = carry
            mid = (lo + hi + 1) >> 1
            pattern = jnp.where(mid >= 0, mid, mid ^ 0x7FFF) << 16
            cand = lax.bitcast_convert_type(pattern, F32).astype(BF16)
            ok = count16(jnp.broadcast_to(cand, (16, tq)), nb) >= topk
            return jnp.where(ok, mid, lo), jnp.where(ok, hi, mid - 1)

        c16, _ = lax.fori_loop(0, 16, coarse_body, (jnp.full((1, tq), KEY_NEG_INF >> 16, jnp.int32),
                                                    jnp.full((1, tq), KEY_POS_INF >> 16, jnp.int32)))
        key_t16 = jnp.where(c16 >= 0, c16 << 16, (c16 << 16) | 0xFFFF)

        def fine_body(_, carry):
            lo, hi = carry
            mid = (lo | hi) - ((lo ^ hi) >> 1)
            cand = _key_to_f32(mid)
            ok = count(lambda blk, b: blk >= cand, nb) >= topk
            return jnp.where(ok, mid, lo), jnp.where(ok, hi, mid - 1)

        lo, _ = lax.fori_loop(0, 17, fine_body, (jnp.maximum(key_t16 - 32770, KEY_NEG_INF),
                                                 jnp.minimum(key_t16 + 65536, KEY_POS_INF)))
        thr = _key_to_f32(lo)
        return thr, count(lambda blk, b: blk > thr, nb), count(lambda blk, b: blk == thr, nb)

    for jj in range(nq):
        @pl.when(j == jj)
        def _(jj=jj):
            thr_s, n_gt_s, n_eq_s = search((jj * tq + tq + kb - 1) // kb)
            thr_scr[...] = jnp.broadcast_to(thr_s, (8, tq))
            cnt_scr[0] = jnp.broadcast_to(n_gt_s, (8, tq))
            cnt_scr[1] = jnp.broadcast_to(n_eq_s, (8, tq))

    thr = thr_scr[0:1, :]
    n_gt = cnt_scr[0, 0:1, :]
    n_eq = cnt_scr[1, 0:1, :]

    def refine_cond(carry):
        return jnp.max(jnp.where(carry[1] >= topk, 1, 0)) > 0

    def refine_body(carry):
        cur, cnt, _ = carry
        above = reduce_keys(lambda blk, b: jnp.where(blk > cur, blk, jnp.inf), jnp.inf,
                            lambda a, m: jnp.minimum(a, jnp.min(m, axis=0)))
        cur = jnp.where(cnt >= topk, jnp.min(above, axis=0, keepdims=True), cur)
        return cur, count(lambda blk, b: blk > cur), count(lambda blk, b: blk == cur)

    thr, n_gt, n_eq = lax.while_loop(refine_cond, refine_body, (thr, n_gt, n_eq))
    need = topk - n_gt
    has_tie = jnp.logical_and(n_eq > need, thr > -jnp.inf)
    j_max = nkb * kb - 1
    j_scr[...] = jnp.full((8, tq), j_max, jnp.int32)

    @pl.when(jnp.max(jnp.where(has_tie, 1, 0)) > 0)
    def _():
        def tie_body(_, carry):
            tlo, thi = carry
            mid = (tlo + thi) >> 1
            ok = count(lambda blk, b: jnp.logical_and(blk == thr, key0 + b * kb <= mid)) >= need
            return jnp.where(ok, tlo, mid + 1), jnp.where(ok, mid, thi)
        tlo, _ = lax.fori_loop(0, n_tie_iter, tie_body,
                               (jnp.zeros((1, tq), jnp.int32), jnp.full((1, tq), j_max, jnp.int32)))
        j_scr[...] = jnp.broadcast_to(jnp.where(has_tie, tlo, j_max), (8, tq))

    j_sel = jnp.where(thr > -jnp.inf, j_scr[0:1, :], -1)

    qall = q_ref[...].reshape(rows, ATT_HEAD_DIM)
    m_scr[...] = jnp.full((rows, half), NEG_BIG, F32)

    def logits_body(b):
        kblk = k_ref[pl.ds(pl.multiple_of(b * kb, kb), kb), :]
        blk = idx_scr[b]
        sel = jnp.logical_or(blk > thr, jnp.logical_and(blk == thr, key0 + b * kb <= j_sel))
        bias = jnp.where(sel, 0.0, NEG_BIG).T
        lg = _nt_dot(qall, kblk).reshape(heads, tq, kb) + bias[None]
        lg_scr[b] = lg
        lg2 = lg.reshape(rows, kb)
        m_scr[...] = jnp.maximum(m_scr[...], jnp.maximum(lg2[:, :half], lg2[:, half:]))

    for_key_blocks(logits_body)
    m_scr[...] = jnp.broadcast_to(jnp.max(m_scr[...], axis=1, keepdims=True), (rows, half))

    acc_scr[...] = jnp.zeros((rows, 2 * ATT_HEAD_DIM), F32)

    def pv_body(b):
        vblk = v_ref[pl.ds(pl.multiple_of(b * kb, kb), kb), :]
        lg2 = lg_scr[b].reshape(rows, kb)
        mrow = m_scr[...]
        p = jnp.concatenate([jnp.exp2(lg2[:, :half] - mrow), jnp.exp2(lg2[:, half:] - mrow)], axis=1).astype(BF16)
        acc_scr[...] += jnp.dot(p, vblk, preferred_element_type=F32)

    for_key_blocks(pv_body)
    acc = acc_scr[...]
    out = acc[:, :ATT_HEAD_DIM] / acc[:, ATT_HEAD_DIM:ATT_HEAD_DIM + 1]
    for hd in range(heads):
        o_ref[:, hd * ATT_HEAD_DIM:(hd + 1) * ATT_HEAD_DIM] = out[hd * tq:(hd + 1) * tq]


def _att_call(q, k, v, qi, ki, wi, *, tq, kb):
    bsz, _, seq, _ = q.shape
    topk = min(TOPK_MAX, seq // TOPK_FRACTION)
    assert tq % CHUNK == 0 and tq >= topk and seq % tq == 0 and seq % kb == 0 and kb % 256 == 0
    nq = seq // tq
    nkb_max = seq // kb
    assert (kb // 16) * nkb_max <= 256
    rows = ATT_HEADS * tq
    kern = functools.partial(_att_kernel, tq=tq, kb=kb, topk=topk, n_tie_iter=max(1, math.ceil(math.log2(seq))),
                             nq=nq)
    return pl.pallas_call(
        kern,
        grid=(bsz, nq),
        in_specs=[
            pl.BlockSpec((None, ATT_HEADS, tq, ATT_HEAD_DIM), lambda b, j: (b, 0, j, 0)),
            pl.BlockSpec((None, seq, ATT_HEAD_DIM), lambda b, j: (b, 0, 0)),
            pl.BlockSpec((None, seq, 2 * ATT_HEAD_DIM), lambda b, j: (b, 0, 0)),
            pl.BlockSpec((None, IDX_HEADS, tq, IDX_DIM), lambda b, j: (b, 0, j, 0)),
            pl.BlockSpec((None, seq, IDX_DIM), lambda b, j: (b, 0, 0)),
            pl.BlockSpec((None, IDX_HEADS, tq), lambda b, j: (b, 0, j)),
        ],
        out_specs=pl.BlockSpec((None, tq, ATT_WIDTH), lambda b, j: (b, j, 0)),
        out_shape=jax.ShapeDtypeStruct((bsz, seq, ATT_WIDTH), F32),
        scratch_shapes=[
            pltpu.VMEM((nkb_max, kb, tq), F32),
            pltpu.VMEM((nkb_max, kb, tq), BF16),
            pltpu.VMEM((nkb_max, ATT_HEADS, tq, kb), F32),
            pltpu.VMEM((rows, kb // 2), F32),
            pltpu.VMEM((rows, 2 * ATT_HEAD_DIM), F32),
            pltpu.VMEM((8, tq), jnp.int32),
            pltpu.VMEM((8, tq), F32),
            pltpu.VMEM((2, 8, tq), F32),
        ],
        compiler_params=_cparams(("arbitrary", "arbitrary")),
        name="dsa_attention",
    )(q, k, v, qi, ki, wi)


def _ssm_kernel(u_ref, kbd_ref, bm_ref, cm_ref, a8_ref, d_ref, wglu_ref, o_ref, st_scr, y_scr, u_scr, *, seq):
    nc = seq // SSM_SUB
    half = SSM_LANES // 2
    u = u_ref[...]
    sub = lax.broadcasted_iota(jnp.int32, (seq, 1), 0) % SSM_SUB
    y = u * d_ref[...] + jnp.dot(u.astype(BF16), kbd_ref[0], preferred_element_type=F32)
    for tau in range(1, SSM_SUB):
        shifted = jnp.where(sub >= tau, pltpu.roll(u, tau, 0), 0.0).astype(BF16)
        y = y + jnp.dot(shifted, kbd_ref[tau], preferred_element_type=F32)
    for hf in range(2):
        y_scr[hf] = y[:, hf * 128:(hf + 1) * 128]
        u_scr[hf] = u[:, hf * 128:(hf + 1) * 128]

    e = jnp.zeros((nc, SSM_LANES), F32)
    for s in range(SSM_SUB):
        rows = pl.ds(s, nc, stride=SSM_SUB)
        us = jnp.concatenate([u_scr[0, rows, :], u_scr[1, rows, :]], axis=1).astype(BF16)
        e = e + jnp.dot(us, bm_ref[s], preferred_element_type=F32)
    st_scr[...] = e

    a8r = a8_ref[:, :half]
    a8i = a8_ref[:, half:]

    def scan_body(c, carry):
        xr, xi = carry
        inc = st_scr[pl.ds(c, 1), :]
        st_scr[pl.ds(c, 1), :] = jnp.concatenate([xr, xi], axis=1)
        return (a8r * xr - a8i * xi + inc[:, :half], a8r * xi + a8i * xr + inc[:, half:])

    lax.fori_loop(0, nc, scan_body, (jnp.zeros((1, half), F32), jnp.zeros((1, half), F32)))

    x_in = st_scr[...].astype(BF16)
    for i in range(SSM_SUB):
        rows = pl.ds(i, nc, stride=SSM_SUB)
        yi = jnp.dot(x_in, cm_ref[i], preferred_element_type=F32)
        for hf in range(2):
            y_scr[hf, rows, :] = y_scr[hf, rows, :] + yi[:, hf * 128:(hf + 1) * 128]

    act = jax.nn.gelu(jnp.concatenate([y_scr[0], y_scr[1]], axis=1)).astype(BF16)
    ag = jnp.dot(act, wglu_ref[...], preferred_element_type=F32)
    o_ref[...] = ag[:, :SSM_WIDTH] * jax.nn.sigmoid(ag[:, SSM_WIDTH:])


def _ssm_call(u, kbd, bm, cm, a8, d, w_glu):
    bsz, seq, _ = u.shape
    assert seq % SSM_SUB == 0
    return pl.pallas_call(
        functools.partial(_ssm_kernel, seq=seq),
        grid=(bsz,),
        in_specs=[
            pl.BlockSpec((None, seq, SSM_WIDTH), lambda b: (b, 0, 0)),
            _resident((SSM_SUB, SSM_WIDTH, SSM_WIDTH)),
            _resident((SSM_SUB, SSM_WIDTH, SSM_LANES)),
            _resident((SSM_SUB, SSM_LANES, SSM_WIDTH)),
            _resident((1, SSM_LANES)),
            _resident((1, SSM_WIDTH)),
            _resident((SSM_WIDTH, 2 * SSM_WIDTH)),
        ],
        out_specs=pl.BlockSpec((None, seq, SSM_WIDTH), lambda b: (b, 0, 0)),
        out_shape=jax.ShapeDtypeStruct((bsz, seq, SSM_WIDTH), F32),
        scratch_shapes=[pltpu.VMEM((seq // SSM_SUB, SSM_LANES), F32), pltpu.VMEM((2, seq, 128), F32),
                        pltpu.VMEM((2, seq, 128), F32)],
        compiler_params=_cparams(("arbitrary",)),
        name="s5_ssm",
    )(u, kbd, bm, cm, a8, d, w_glu)


def _ssm_params(a_re, a_im, log_dt, b_re, b_im, c_re, c_im, d_skip):
    dt = jnp.exp(log_dt)[:, None]
    mag = jnp.exp(dt * a_re)
    abr, abi = mag * jnp.cos(dt * a_im), mag * jnp.sin(dt * a_im)
    den = a_re * a_re + a_im * a_im
    nr, ni = abr - 1.0, abi
    qr, qi = (nr * a_re + ni * a_im) / den, (ni * a_re - nr * a_im) / den
    bbr = qr[..., None] * b_re - qi[..., None] * b_im
    bbi = qr[..., None] * b_im + qi[..., None] * b_re
    pr, pi = [jnp.ones_like(abr)], [jnp.zeros_like(abr)]
    for _ in range(SSM_SUB):
        pr, pi = pr + [pr[-1] * abr - pi[-1] * abi], pi + [pr[-1] * abi + pi[-1] * abr]
    pwr, pwi = jnp.stack(pr), jnp.stack(pi)
    eye = jnp.eye(SSM_GROUPS, dtype=F32)
    abr_t = pwr[:SSM_SUB, :, :, None] * bbr[None] - pwi[:SSM_SUB, :, :, None] * bbi[None]
    abi_t = pwr[:SSM_SUB, :, :, None] * bbi[None] + pwi[:SSM_SUB, :, :, None] * bbr[None]
    kt = (jnp.einsum('gop,tgpc->tgco', c_re, abr_t, precision=HI)
          - jnp.einsum('gop,tgpc->tgco', c_im, abi_t, precision=HI))
    kbd = jnp.einsum('tgco,gh->tgcho', kt, eye).reshape(SSM_SUB, SSM_WIDTH, SSM_WIDTH)
    rev_r = jnp.stack([abr_t[SSM_SUB - 1 - s] for s in range(SSM_SUB)])
    rev_i = jnp.stack([abi_t[SSM_SUB - 1 - s] for s in range(SSM_SUB)])
    bm_r = jnp.einsum('sgpc,gh->sgchp', rev_r, eye).reshape(SSM_SUB, SSM_WIDTH, SSM_GROUPS * SSM_STATE)
    bm_i = jnp.einsum('sgpc,gh->sgchp', rev_i, eye).reshape(SSM_SUB, SSM_WIDTH, SSM_GROUPS * SSM_STATE)
    bm = jnp.concatenate([bm_r, bm_i], axis=-1)
    ar1, ai1 = pwr[1:, :, None, :], pwi[1:, :, None, :]
    mr = c_re[None] * ar1 - c_im[None] * ai1
    mi = c_re[None] * ai1 + c_im[None] * ar1
    cm_r = jnp.einsum('igop,gh->igpho', mr, eye).reshape(SSM_SUB, SSM_GROUPS * SSM_STATE, SSM_WIDTH)
    cm_i = jnp.einsum('igop,gh->igpho', -mi, eye).reshape(SSM_SUB, SSM_GROUPS * SSM_STATE, SSM_WIDTH)
    cm = jnp.concatenate([cm_r, cm_i], axis=1)
    a8 = jnp.concatenate([pwr[SSM_SUB].reshape(1, -1), pwi[SSM_SUB].reshape(1, -1)], axis=1)
    return kbd.astype(BF16), bm.astype(BF16), cm.astype(BF16), a8, d_skip.reshape(1, SSM_WIDTH)


def _conv_kernel(cur_ref, halo_ref, wdw_ref, bdw_ref, lng_ref, lnb_ref, wpw_ref, o_ref, pad_scr, *, rb):
    t = pl.program_id(1)

    def glu(blk):
        return blk[:, :CONV_WIDTH] * jax.nn.sigmoid(blk[:, CONV_WIDTH:])

    pad_scr[0:CONV_HALO, :] = glu(halo_ref[...]) * (t > 0).astype(F32)
    pad_scr[CONV_HALO:CONV_HALO + rb, :] = glu(cur_ref[...])
    pad_scr[CONV_HALO + rb:, :] = jnp.zeros((8, CONV_WIDTH), F32)
    y = jnp.zeros((rb, CONV_WIDTH), F32) + bdw_ref[...]
    for phase in range(8):
        part = None
        for tap in range(CONV_KSIZE):
            off = CONV_HALO - (CONV_KSIZE - 1) + tap
            if off % 8 != phase:
                continue
            term = pad_scr[off - phase:off - phase + rb + 8, :] * wdw_ref[tap:tap + 1, :]
            part = term if part is None else part + term
        y = y + part[phase:phase + rb, :]
    mu = jnp.mean(y, axis=-1, keepdims=True)
    var = jnp.mean(jnp.square(y - mu), axis=-1, keepdims=True)
    z = (y - mu) * lax.rsqrt(var + NORM_EPS) * lng_ref[...] + lnb_ref[...]
    z = (z * jax.nn.sigmoid(z)).astype(BF16)
    o_ref[...] = jnp.dot(z, wpw_ref[...], preferred_element_type=F32)


def _conv_call(uc, w_dw, b_dw, ln_g, ln_b, w_pw, *, rb):
    bsz, seq, _ = uc.shape
    assert seq % rb == 0 and rb % CONV_HALO == 0
    per = rb // CONV_HALO
    return pl.pallas_call(
        functools.partial(_conv_kernel, rb=rb),
        grid=(bsz, seq // rb),
        in_specs=[
            pl.BlockSpec((None, rb, 2 * CONV_WIDTH), lambda b, t: (b, t, 0)),
            pl.BlockSpec((None, CONV_HALO, 2 * CONV_WIDTH), lambda b, t: (b, jnp.maximum(t * per - 1, 0), 0)),
            _resident((CONV_KSIZE, CONV_WIDTH)),
            _resident((1, CONV_WIDTH)),
            _resident((1, CONV_WIDTH)),
            _resident((1, CONV_WIDTH)),
            _resident((CONV_WIDTH, CONV_WIDTH)),
        ],
        out_specs=pl.BlockSpec((None, rb, CONV_WIDTH), lambda b, t: (b, t, 0)),
        out_shape=jax.ShapeDtypeStruct((bsz, seq, CONV_WIDTH), F32),
        scratch_shapes=[pltpu.VMEM((rb + CONV_HALO + 8, CONV_WIDTH), F32)],
        compiler_params=_cparams(("arbitrary", "arbitrary")),
        name="conformer_conv",
    )(uc, uc, w_dw, b_dw, ln_g, ln_b, w_pw)


def _reorder_w_in(w_in):
    splits = [0]
    for c in IN_COLS:
        splits.append(splits[-1] + c)
    q, k, v, qi, ki, wi, us, uc = [w_in[:, splits[i]:splits[i + 1]] for i in range(len(IN_COLS))]
    pad = jnp.zeros((w_in.shape[0], PROJ_SSM0 - PROJ_KW0 - IDX_DIM - IDX_HEADS), w_in.dtype)
    return jnp.concatenate([q, k, v, qi, ki, wi, pad, us, uc], axis=1).astype(BF16)


def kernel(x, c, mod_w, mod_b, ffn1_norm, ffn1_w_gu, ffn1_w_down, mix_norm, w_in, w_out, att_out_norm, ssm_out_norm, conv_out_norm, ssm_a_re, ssm_a_im, ssm_log_dt, ssm_b_re, ssm_b_im, ssm_c_re, ssm_c_im, ssm_d, ssm_w_glu, conv_w_dw, conv_b_dw, conv_ln_g, conv_ln_b, conv_w_pw, ffn2_norm, ffn2_w_gu, ffn2_w_down, final_norm):
    bsz, seq, _ = x.shape
    depth = mod_w.shape[0]
    tm = min(512, seq)
    tq = min(256, seq)
    mod = _mod_call(c, mod_w, mod_b).reshape(depth, bsz, N_MOD, D_MODEL)
    row = lambda v: v.reshape(1, -1)
    fin = row(final_norm)
    for l in range(depth):
        mod3 = mod[l]
        wi0 = sum(IN_COLS[:5])
        w_wi_t = w_in[l][:, wi0:wi0 + IDX_HEADS].T.astype(BF16)
        x, q, k, v, qi, ki, wi, us, uc = _ffn_proj_call(
            x, mod3, row(ffn1_norm[l]), ffn1_w_gu[l].astype(BF16), ffn1_w_down[l].astype(BF16), fin,
            row(mix_norm[l]), _reorder_w_in(w_in[l]), w_wi_t, tm=tm)
        y_att = _att_call(q, k, v, qi, ki, wi, tq=tq, kb=256)
        y_ssm = _ssm_call(us, *_ssm_params(ssm_a_re[l], ssm_a_im[l], ssm_log_dt[l], ssm_b_re[l], ssm_b_im[l],
                                           ssm_c_re[l], ssm_c_im[l], ssm_d[l]), ssm_w_glu[l].astype(BF16))
        y_conv = _conv_call(uc, conv_w_dw[l], row(conv_b_dw[l]), row(conv_ln_g[l]), row(conv_ln_b[l]),
                            conv_w_pw[l].astype(BF16), rb=min(256, seq))
        x = _mix_ffn_call(x, mod3, y_att, y_ssm, y_conv, row(att_out_norm[l]), row(ssm_out_norm[l]),
                          row(conv_out_norm[l]), w_out[l].astype(BF16), row(ffn2_norm[l]),
                          ffn2_w_gu[l].astype(BF16), ffn2_w_down[l].astype(BF16), fin,
                          final=(l == depth - 1), tm=tm)
    return x
```

```python
import functools
import math

import jax
import jax.numpy as jnp
from jax import lax
from jax.experimental import pallas as pl
from jax.experimental.pallas import tpu as pltpu

F32 = jnp.float32
BF16 = jnp.bfloat16

D_MODEL = 1024
CHUNK = 64
ATT_HEADS = 8
ATT_HEAD_DIM = 64
ATT_WIDTH = ATT_HEADS * ATT_HEAD_DIM
IDX_HEADS = 8
IDX_DIM = 32
TOPK_MAX = 256
TOPK_FRACTION = 4
SSM_WIDTH = 256
SSM_GROUP = 16
SSM_GROUPS = 16
SSM_STATE = 64
SSM_LANES = 2 * SSM_GROUPS * SSM_STATE
SSM_SUB = 8
CONV_WIDTH = 256
CONV_KSIZE = 31
CONV_HALO = 32
D_FF = 2816
N_MOD = 9
NORM_EPS = 1e-6
IN_COLS = (ATT_WIDTH, ATT_HEAD_DIM, ATT_HEAD_DIM, IDX_HEADS * IDX_DIM, IDX_DIM, IDX_HEADS, SSM_WIDTH, 2 * CONV_WIDTH)

VMEM_LIMIT = 56 * 1024 * 1024
NEG_BIG = -1e30
LOG2_E = 1.4426950408889634
KEY_NEG_INF = -2139095041
KEY_POS_INF = 2139095040
HI = lax.Precision.HIGHEST


def _cparams(sem):
    return pltpu.CompilerParams(dimension_semantics=sem, vmem_limit_bytes=VMEM_LIMIT)


def _resident(shape):
    nd = len(shape)
    return pl.BlockSpec(shape, lambda *_: (0,) * nd, pipeline_mode=pl.Buffered(1))


def _nt_dot(a, b):
    return lax.dot_general(a, b, (((1,), (1,)), ((), ())), preferred_element_type=F32)


def _modulated_norm(x, gain, shift, scale):
    y = x * lax.rsqrt(jnp.mean(x * x, axis=-1, keepdims=True) + NORM_EPS) * gain
    return y * (1.0 + scale) + shift


def _mod_kernel(c_ref, w_ref, b_ref, o_ref):
    c = c_ref[...]
    cond = c * jax.nn.sigmoid(c)
    o_ref[...] = jnp.dot(cond, w_ref[...], preferred_element_type=F32, precision=HI) + b_ref[...]


def _mod_call(c, mod_w, mod_b):
    depth = mod_w.shape[0]
    bsz = c.shape[0]
    return pl.pallas_call(
        _mod_kernel,
        grid=(depth, N_MOD),
        in_specs=[
            pl.BlockSpec((bsz, D_MODEL), lambda l, j: (0, 0)),
            pl.BlockSpec((None, D_MODEL, D_MODEL), lambda l, j: (l, 0, j)),
            pl.BlockSpec((None, 1, D_MODEL), lambda l, j: (l, 0, j)),
        ],
        out_specs=pl.BlockSpec((None, bsz, D_MODEL), lambda l, j: (l, 0, j)),
        out_shape=jax.ShapeDtypeStruct((depth, bsz, N_MOD * D_MODEL), F32),
        compiler_params=_cparams(("arbitrary", "arbitrary")),
        name="adaln_mod",
    )(c, mod_w, mod_b.reshape(depth, 1, N_MOD * D_MODEL))


FFN_CHUNK = 256


def _rms_gain(y, gain):
    return (y * lax.rsqrt(jnp.mean(y * y, axis=-1, keepdims=True) + NORM_EPS) * gain).astype(BF16)


def _mix_ffn_kernel(x_ref, mod_ref, ya_ref, ys_ref, yc_ref, ga_ref, gs_ref, gc_ref, wo_ref,
                    ng_ref, wgu_ref, wd_ref, fn_ref, o_ref, *, final):
    mix = jnp.dot(_rms_gain(ya_ref[...], ga_ref[...]), wo_ref[0:ATT_WIDTH, :], preferred_element_type=F32)
    mix = mix + jnp.dot(_rms_gain(ys_ref[...], gs_ref[...]), wo_ref[ATT_WIDTH:ATT_WIDTH + SSM_WIDTH, :],
                        preferred_element_type=F32)
    mix = mix + jnp.dot(_rms_gain(yc_ref[...], gc_ref[...]), wo_ref[ATT_WIDTH + SSM_WIDTH:, :],
                        preferred_element_type=F32)
    x = x_ref[...] + mod_ref[5:6, :] * mix
    o_ref[...] = _ffn_math(x, mod_ref, ng_ref, wgu_ref, wd_ref, fn_ref, row0=6, final=final)


def _ffn_math(x, mod_ref, ng_ref, wgu_ref, wd_ref, fn_ref, *, row0, final):
    h = _modulated_norm(x, ng_ref[...], mod_ref[row0:row0 + 1, :], mod_ref[row0 + 1:row0 + 2, :]).astype(BF16)
    acc = jnp.zeros(x.shape, F32)
    for c0 in range(0, D_FF, FFN_CHUNK):
        gate = jnp.dot(h, wgu_ref[:, c0:c0 + FFN_CHUNK], preferred_element_type=F32)
        up = jnp.dot(h, wgu_ref[:, D_FF + c0:D_FF + c0 + FFN_CHUNK], preferred_element_type=F32)
        a = (gate * jax.nn.sigmoid(gate) * up).astype(BF16)
        acc = acc + jnp.dot(a, wd_ref[c0:c0 + FFN_CHUNK, :], preferred_element_type=F32)
    out = x + (0.5 * mod_ref[row0 + 2:row0 + 3, :]) * acc
    if final:
        out = out * lax.rsqrt(jnp.mean(out * out, axis=-1, keepdims=True) + NORM_EPS) * fn_ref[...]
    return out


def _mix_ffn_call(x, mod3, y_att, y_ssm, y_conv, g_att, g_ssm, g_conv, w_out, norm_g, w_gu, w_down, final_g,
                  *, final, tm):
    bsz, seq, _ = x.shape
    tok = lambda width: pl.BlockSpec((None, tm, width), lambda b, t: (b, t, 0))
    return pl.pallas_call(
        functools.partial(_mix_ffn_kernel, final=final),
        grid=(bsz, seq // tm),
        in_specs=[
            tok(D_MODEL),
            pl.BlockSpec((None, N_MOD, D_MODEL), lambda b, t: (b, 0, 0)),
            tok(ATT_WIDTH), tok(SSM_WIDTH), tok(CONV_WIDTH),
            _resident((1, ATT_WIDTH)), _resident((1, SSM_WIDTH)), _resident((1, CONV_WIDTH)),
            _resident((D_MODEL, D_MODEL)),
            _resident((1, D_MODEL)),
            _resident((D_MODEL, 2 * D_FF)),
            _resident((D_FF, D_MODEL)),
            _resident((1, D_MODEL)),
        ],
        out_specs=tok(D_MODEL),
        out_shape=jax.ShapeDtypeStruct(x.shape, F32),
        compiler_params=_cparams(("arbitrary", "arbitrary")),
        name="mix_ffn_final" if final else "mix_ffn",
    )(x, mod3, y_att, y_ssm, y_conv, g_att, g_ssm, g_conv, w_out, norm_g, w_gu, w_down, final_g)


PROJ_Q0, PROJ_KV0, PROJ_QI0, PROJ_KW0, PROJ_SSM0, PROJ_CONV0, PROJ_END = 0, 512, 640, 896, 1024, 1280, 1792


def _ffn_proj_kernel(x_ref, mod_ref, fng_ref, wgu_ref, wd_ref, fn_ref, ng_ref, w_ref, wit_ref,
                     xo_ref, q_ref, k_ref, v_ref, qi_ref, ki_ref, wi_ref, us_ref, uc_ref):
    x = _ffn_math(x_ref[...], mod_ref, fng_ref, wgu_ref, wd_ref, fn_ref, row0=0, final=False)
    xo_ref[...] = x
    h = _modulated_norm(x, ng_ref[...], mod_ref[3:4, :], mod_ref[4:5, :]).astype(BF16)
    q = jnp.dot(h, w_ref[:, PROJ_Q0:PROJ_KV0], preferred_element_type=F32) * (ATT_HEAD_DIM ** -0.5 * LOG2_E)
    for hd in range(ATT_HEADS):
        q_ref[hd] = q[:, hd * ATT_HEAD_DIM:(hd + 1) * ATT_HEAD_DIM].astype(BF16)
    kv = jnp.dot(h, w_ref[:, PROJ_KV0:PROJ_QI0], preferred_element_type=F32)
    k_ref[...] = kv[:, :ATT_HEAD_DIM].astype(BF16)
    lane = lax.broadcasted_iota(jnp.int32, kv.shape, 1)
    ones_col = jnp.where(lane == ATT_HEAD_DIM, 1.0, 0.0)
    v_ref[...] = jnp.where(lane < ATT_HEAD_DIM, pltpu.roll(kv, ATT_HEAD_DIM, 1), ones_col).astype(BF16)
    qi = jnp.dot(h, w_ref[:, PROJ_QI0:PROJ_KW0], preferred_element_type=F32)
    for hd in range(IDX_HEADS):
        qi_ref[hd] = qi[:, hd * IDX_DIM:(hd + 1) * IDX_DIM].astype(BF16)
    kw = jnp.dot(h, w_ref[:, PROJ_KW0:PROJ_SSM0], preferred_element_type=F32)
    ki_ref[...] = kw[:, :IDX_DIM].astype(BF16)
    wi_ref[...] = _nt_dot(wit_ref[...], h) * (IDX_DIM ** -0.5 * IDX_HEADS ** -0.5)
    us_ref[...] = jnp.dot(h, w_ref[:, PROJ_SSM0:PROJ_CONV0], preferred_element_type=F32)
    uc_ref[...] = jnp.dot(h, w_ref[:, PROJ_CONV0:PROJ_END], preferred_element_type=F32)


def _ffn_proj_call(x, mod3, ffn_norm_g, w_gu, w_down, final_g, norm_g, w_in_r, w_wi_t, *, tm):
    bsz, seq, _ = x.shape
    nt = seq // tm
    tok = lambda width: pl.BlockSpec((None, tm, width), lambda b, t: (b, t, 0))
    head = lambda n, width: pl.BlockSpec((None, n, tm, width), lambda b, t: (b, 0, t, 0))
    return pl.pallas_call(
        _ffn_proj_kernel,
        grid=(bsz, nt),
        in_specs=[
            tok(D_MODEL),
            pl.BlockSpec((None, N_MOD, D_MODEL), lambda b, t: (b, 0, 0)),
            _resident((1, D_MODEL)),
            _resident((D_MODEL, 2 * D_FF)),
            _resident((D_FF, D_MODEL)),
            _resident((1, D_MODEL)),
            _resident((1, D_MODEL)),
            _resident((D_MODEL, PROJ_END)),
            _resident((IDX_HEADS, D_MODEL)),
        ],
        out_specs=[
            tok(D_MODEL),
            head(ATT_HEADS, ATT_HEAD_DIM), tok(ATT_HEAD_DIM), tok(2 * ATT_HEAD_DIM),
            head(IDX_HEADS, IDX_DIM), tok(IDX_DIM),
            pl.BlockSpec((None, IDX_HEADS, tm), lambda b, t: (b, 0, t)),
            tok(SSM_WIDTH), tok(2 * CONV_WIDTH),
        ],
        out_shape=[
            jax.ShapeDtypeStruct(x.shape, F32),
            jax.ShapeDtypeStruct((bsz, ATT_HEADS, seq, ATT_HEAD_DIM), BF16),
            jax.ShapeDtypeStruct((bsz, seq, ATT_HEAD_DIM), BF16),
            jax.ShapeDtypeStruct((bsz, seq, 2 * ATT_HEAD_DIM), BF16),
            jax.ShapeDtypeStruct((bsz, IDX_HEADS, seq, IDX_DIM), BF16),
            jax.ShapeDtypeStruct((bsz, seq, IDX_DIM), BF16),
            jax.ShapeDtypeStruct((bsz, IDX_HEADS, seq), F32),
            jax.ShapeDtypeStruct((bsz, seq, SSM_WIDTH), F32),
            jax.ShapeDtypeStruct((bsz, seq, 2 * CONV_WIDTH), F32),
        ],
        compiler_params=_cparams(("arbitrary", "arbitrary")),
        name="ffn_in_proj",
    )(x, mod3, ffn_norm_g, w_gu, w_down, final_g, norm_g, w_in_r, w_wi_t)


def _key_to_f32(key):
    return lax.bitcast_convert_type(jnp.where(key >= 0, key, key ^ 0x7FFFFFFF), F32)


def _att_kernel(q_ref, k_ref, v_ref, qi_ref, ki_ref, wi_ref, o_ref,
                idx_scr, xb_scr, lg_scr, m_scr, acc_scr, j_scr, thr_scr, cnt_scr, *, tq, kb, topk, n_tie_iter, nq):
    j = pl.program_id(1)
    nkb = (j * tq + tq + kb - 1) // kb
    heads = ATT_HEADS
    rows = heads * tq
    half = kb // 2

    qcol = lax.broadcasted_iota(jnp.int32, (1, tq), 1)
    limit = j * tq + ((qcol // CHUNK) + 1) * CHUNK
    key0 = lax.broadcasted_iota(jnp.int32, (kb, tq), 0)
    wt = wi_ref[...]

    def for_key_blocks(body):
        n4 = nkb // 4

        def quad(i, carry):
            for u in range(4):
                body(i * 4 + u)
            return carry

        lax.fori_loop(0, n4, quad, 0)
        rem = nkb - n4 * 4

        @pl.when((rem & 2) != 0)
        def _():
            body(n4 * 4)
            body(n4 * 4 + 1)

        @pl.when((rem & 1) != 0)
        def _():
            body(n4 * 4 + (rem & 2))

    def idx_body(b):
        kib = ki_ref[pl.ds(pl.multiple_of(b * kb, kb), kb), :]
        acc = jnp.zeros((kb, tq), F32)
        for hd in range(IDX_HEADS):
            acc = acc + jnp.maximum(_nt_dot(kib, qi_ref[hd]), 0.0) * wt[hd:hd + 1, :]
        score = jnp.where(key0 + b * kb < limit, acc, -jnp.inf)
        idx_scr[b] = score
        xb_scr[b] = score.astype(BF16)

    for_key_blocks(idx_body)

    def over_blocks(nb, body, init):
        if isinstance(nb, int):
            acc = init
            for b in range(nb):
                acc = body(b, acc)
            return acc
        return lax.fori_loop(0, nb, body, init)

    def reduce_keys(fn, init, combine, nb=nkb):
        def body(b, acc):
            return combine(acc, fn(idx_scr[b], b).reshape(kb // 8, 8, tq))
        return over_blocks(nb, body, jnp.full((8, tq), init, F32))

    def count(pred, nb=nkb):
        acc = reduce_keys(lambda blk, b: jnp.where(pred(blk, b), 1.0, 0.0), 0.0,
                          lambda a, m: a + jnp.sum(m, axis=0), nb)
        return jnp.sum(acc, axis=0, keepdims=True)

    one_b, zero_b = jnp.ones((16, tq), BF16), jnp.zeros((16, tq), BF16)

    def count16(cand_b, nb):
        def body(b, acc):
            blk = xb_scr[b]
            for i in range(kb // 16):
                acc = acc + jnp.where(blk[16 * i:16 * (i + 1)] >= cand_b, one_b, zero_b)
            return acc
        return jnp.sum(over_blocks(nb, body, zero_b).astype(F32), axis=0, keepdims=True)

    def search(nb):
        def coarse_body(_, carry):
            lo, hi = carry
            mid = (lo + hi + 1) >> 1
            pattern = jnp.where(mid >= 0, mid, mid ^ 0x7FFF) << 16
            cand = lax.bitcast_convert_type(pattern, F32).astype(BF16)
            ok = count16(jnp.broadcast_to(cand, (16, tq)), nb) >= topk
            return jnp.where(ok, mid, lo), jnp.where(ok, hi, mid - 1)

        c16, _ = lax.fori_loop(0, 16, coarse_body, (jnp.full((1, tq), KEY_NEG_INF >> 16, jnp.int32),
                                                    jnp.full((1, tq), KEY_POS_INF >> 16, jnp.int32)))
        key_t16 = jnp.where(c16 >= 0, c16 << 16, (c16 << 16) | 0xFFFF)

        def fine_body(_, carry):
            lo, hi = carry
            mid = (lo | hi) - ((lo ^ hi) >> 1)
            cand = _key_to_f32(mid)
            ok = count(lambda blk, b: blk >= cand, nb) >= topk
            return jnp.where(ok, mid, lo), jnp.where(ok, hi, mid - 1)

        lo, _ = lax.fori_loop(0, 17, fine_body, (jnp.maximum(key_t16 - 32770, KEY_NEG_INF),
                                                 jnp.minimum(key_t16 + 65536, KEY_POS_INF)))
        thr = _key_to_f32(lo)
        return thr, count(lambda blk, b: blk > thr, nb), count(lambda blk, b: blk == thr, nb)

    for jj in range(nq):
        @pl.when(j == jj)
        def _(jj=jj):
            thr_s, n_gt_s, n_eq_s = search((jj * tq + tq + kb - 1) // kb)
            thr_scr[...] = jnp.broadcast_to(thr_s, (8, tq))
            cnt_scr[0] = jnp.broadcast_to(n_gt_s, (8, tq))
            cnt_scr[1] = jnp.broadcast_to(n_eq_s, (8, tq))

    thr = thr_scr[0:1, :]
    n_gt = cnt_scr[0, 0:1, :]
    n_eq = cnt_scr[1, 0:1, :]

    def refine_cond(carry):
        return jnp.max(jnp.where(carry[1] >= topk, 1, 0)) > 0

    def refine_body(carry):
        cur, cnt, _ = carry
        above = reduce_keys(lambda blk, b: jnp.where(blk > cur, blk, jnp.inf), jnp.inf,
                            lambda a, m: jnp.minimum(a, jnp.min(m, axis=0)))
        cur = jnp.where(cnt >= topk, jnp.min(above, axis=0, keepdims=True), cur)
        return cur, count(lambda blk, b: blk > cur), count(lambda blk, b: blk == cur)

    thr, n_gt, n_eq = lax.while_loop(refine_cond, refine_body, (thr, n_gt, n_eq))
    need = topk - n_gt
    has_tie = jnp.logical_and(n_eq > need, thr > -jnp.inf)
    j_max = nkb * kb - 1
    j_scr[...] = jnp.full((8, tq), j_max, jnp.int32)

    @pl.when(jnp.max(jnp.where(has_tie, 1, 0)) > 0)
    def _():
        def tie_body(_, carry):
            tlo, thi = carry
            mid = (tlo + thi) >> 1
            ok = count(lambda blk, b: jnp.logical_and(blk == thr, key0 + b * kb <= mid)) >= need
            return jnp.where(ok, tlo, mid + 1), jnp.where(ok, mid, thi)
        tlo, _ = lax.fori_loop(0, n_tie_iter, tie_body,
                               (jnp.zeros((1, tq), jnp.int32), jnp.full((1, tq), j_max, jnp.int32)))
        j_scr[...] = jnp.broadcast_to(jnp.where(has_tie, tlo, j_max), (8, tq))

    j_sel = jnp.where(thr > -jnp.inf, j_scr[0:1, :], -1)

    qall = q_ref[...].reshape(rows, ATT_HEAD_DIM)
    m_scr[...] = jnp.full((rows, half), NEG_BIG, F32)

    def logits_body(b):
        kblk = k_ref[pl.ds(pl.multiple_of(b * kb, kb), kb), :]
        blk = idx_scr[b]
        sel = jnp.logical_or(blk > thr, jnp.logical_and(blk == thr, key0 + b * kb <= j_sel))
        bias = jnp.where(sel, 0.0, NEG_BIG).T
        lg = _nt_dot(qall, kblk).reshape(heads, tq, kb) + bias[None]
        lg_scr[b] = lg
        lg2 = lg.reshape(rows, kb)
        m_scr[...] = jnp.maximum(m_scr[...], jnp.maximum(lg2[:, :half], lg2[:, half:]))

    for_key_blocks(logits_body)
    m_scr[...] = jnp.broadcast_to(jnp.max(m_scr[...], axis=1, keepdims=True), (rows, half))

    acc_scr[...] = jnp.zeros((rows, 2 * ATT_HEAD_DIM), F32)

    def pv_body(b):
        vblk = v_ref[pl.ds(pl.multiple_of(b * kb, kb), kb), :]
        lg2 = lg_scr[b].reshape(rows, kb)
        mrow = m_scr[...]
        p = jnp.concatenate([jnp.exp2(lg2[:, :half] - mrow), jnp.exp2(lg2[:, half:] - mrow)], axis=1).astype(BF16)
        acc_scr[...] += jnp.dot(p, vblk, preferred_element_type=F32)

    for_key_blocks(pv_body)
    acc = acc_scr[...]
    out = acc[:, :ATT_HEAD_DIM] / acc[:, ATT_HEAD_DIM:ATT_HEAD_DIM + 1]
    for hd in range(heads):
        o_ref[:, hd * ATT_HEAD_DIM:(hd + 1) * ATT_HEAD_DIM] = out[hd * tq:(hd + 1) * tq]


def _att_call(q, k, v, qi, ki, wi, *, tq, kb):
    bsz, _, seq, _ = q.shape
    topk = min(TOPK_MAX, seq // TOPK_FRACTION)
    assert tq % CHUNK == 0 and tq >= topk and seq % tq == 0 and seq % kb == 0 and kb % 256 == 0
    nq = seq // tq
    nkb_max = seq // kb
    assert (kb // 16) * nkb_max <= 256
    rows = ATT_HEADS * tq
    kern = functools.partial(_att_kernel, tq=tq, kb=kb, topk=topk, n_tie_iter=max(1, math.ceil(math.log2(seq))),
                             nq=nq)
    return pl.pallas_call(
        kern,
        grid=(bsz, nq),
        in_specs=[
            pl.BlockSpec((None, ATT_HEADS, tq, ATT_HEAD_DIM), lambda b, j: (b, 0, j, 0)),
            pl.BlockSpec((None, seq, ATT_HEAD_DIM), lambda b, j: (b, 0, 0)),
            pl.BlockSpec((None, seq, 2 * ATT_HEAD_DIM), lambda b, j: (b, 0, 0)),
            pl.BlockSpec((None, IDX_HEADS, tq, IDX_DIM), lambda b, j: (b, 0, j, 0)),
            pl.BlockSpec((None, seq, IDX_DIM), lambda b, j: (b, 0, 0)),
            pl.BlockSpec((None, IDX_HEADS, tq), lambda b, j: (b, 0, j)),
        ],
        out_specs=pl.BlockSpec((None, tq, ATT_WIDTH), lambda b, j: (b, j, 0)),
        out_shape=jax.ShapeDtypeStruct((bsz, seq, ATT_WIDTH), F32),
        scratch_shapes=[
            pltpu.VMEM((nkb_max, kb, tq), F32),
            pltpu.VMEM((nkb_max, kb, tq), BF16),
            pltpu.VMEM((nkb_max, ATT_HEADS, tq, kb), F32),
            pltpu.VMEM((rows, kb // 2), F32),
            pltpu.VMEM((rows, 2 * ATT_HEAD_DIM), F32),
            pltpu.VMEM((8, tq), jnp.int32),
            pltpu.VMEM((8, tq), F32),
            pltpu.VMEM((2, 8, tq), F32),
        ],
        compiler_params=_cparams(("arbitrary", "arbitrary")),
        name="dsa_attention",
    )(q, k, v, qi, ki, wi)


def _ssm_kernel(u_ref, kbd_ref, bm_ref, cm_ref, a8_ref, d_ref, wglu_ref, o_ref, st_scr, y_scr, u_scr, *, seq):
    nc = seq // SSM_SUB
    half = SSM_LANES // 2
    u = u_ref[...]
    sub = lax.broadcasted_iota(jnp.int32, (seq, 1), 0) % SSM_SUB
    y = u * d_ref[...] + jnp.dot(u.astype(BF16), kbd_ref[0], preferred_element_type=F32)
    for tau in range(1, SSM_SUB):
        shifted = jnp.where(sub >= tau, pltpu.roll(u, tau, 0), 0.0).astype(BF16)
        y = y + jnp.dot(shifted, kbd_ref[tau], preferred_element_type=F32)
    for hf in range(2):
        y_scr[hf] = y[:, hf * 128:(hf + 1) * 128]
        u_scr[hf] = u[:, hf * 128:(hf + 1) * 128]

    e = jnp.zeros((nc, SSM_LANES), F32)
    for s in range(SSM_SUB):
        rows = pl.ds(s, nc, stride=SSM_SUB)
        us = jnp.concatenate([u_scr[0, rows, :], u_scr[1, rows, :]], axis=1).astype(BF16)
        e = e + jnp.dot(us, bm_ref[s], preferred_element_type=F32)
    st_scr[...] = e

    a8r = a8_ref[:, :half]
    a8i = a8_ref[:, half:]

    def scan_body(c, carry):
        xr, xi = carry
        inc = st_scr[pl.ds(c, 1), :]
        st_scr[pl.ds(c, 1), :] = jnp.concatenate([xr, xi], axis=1)
        return (a8r * xr - a8i * xi + inc[:, :half], a8r * xi + a8i * xr + inc[:, half:])

    lax.fori_loop(0, nc, scan_body, (jnp.zeros((1, half), F32), jnp.zeros((1, half), F32)))

    x_in = st_scr[...].astype(BF16)
    for i in range(SSM_SUB):
        rows = pl.ds(i, nc, stride=SSM_SUB)
        yi = jnp.dot(x_in, cm_ref[i], preferred_element_type=F32)
        for hf in range(2):
            y_scr[hf, rows, :] = y_scr[hf, rows, :] + yi[:, hf * 128:(hf + 1) * 128]

    act = jax.nn.gelu(jnp.concatenate([y_scr[0], y_scr[1]], axis=1)).astype(BF16)
    ag = jnp.dot(act, wglu_ref[...], preferred_element_type=F32)
    o_ref[...] = ag[:, :SSM_WIDTH] * jax.nn.sigmoid(ag[:, SSM_WIDTH:])


def _ssm_call(u, kbd, bm, cm, a8, d, w_glu):
    bsz, seq, _ = u.shape
    assert seq % SSM_SUB == 0
    return pl.pallas_call(
        functools.partial(_ssm_kernel, seq=seq),
        grid=(bsz,),
        in_specs=[
            pl.BlockSpec((None, seq, SSM_WIDTH), lambda b: (b, 0, 0)),
            _resident((SSM_SUB, SSM_WIDTH, SSM_WIDTH)),
            _resident((SSM_SUB, SSM_WIDTH, SSM_LANES)),
            _resident((SSM_SUB, SSM_LANES, SSM_WIDTH)),
            _resident((1, SSM_LANES)),
            _resident((1, SSM_WIDTH)),
            _resident((SSM_WIDTH, 2 * SSM_WIDTH)),
        ],
        out_specs=pl.BlockSpec((None, seq, SSM_WIDTH), lambda b: (b, 0, 0)),
        out_shape=jax.ShapeDtypeStruct((bsz, seq, SSM_WIDTH), F32),
        scratch_shapes=[pltpu.VMEM((seq // SSM_SUB, SSM_LANES), F32), pltpu.VMEM((2, seq, 128), F32),
                        pltpu.VMEM((2, seq, 128), F32)],
        compiler_params=_cparams(("arbitrary",)),
        name="s5_ssm",
    )(u, kbd, bm, cm, a8, d, w_glu)


def _ssm_params(a_re, a_im, log_dt, b_re, b_im, c_re, c_im, d_skip):
    dt = jnp.exp(log_dt)[:, None]
    mag = jnp.exp(dt * a_re)
    abr, abi = mag * jnp.cos(dt * a_im), mag * jnp.sin(dt * a_im)
    den = a_re * a_re + a_im * a_im
    nr, ni = abr - 1.0, abi
    qr, qi = (nr * a_re + ni * a_im) / den, (ni * a_re - nr * a_im) / den
    bbr = qr[..., None] * b_re - qi[..., None] * b_im
    bbi = qr[..., None] * b_im + qi[..., None] * b_re
    pr, pi = [jnp.ones_like(abr)], [jnp.zeros_like(abr)]
    for _ in range(SSM_SUB):
        pr, pi = pr + [pr[-1] * abr - pi[-1] * abi], pi + [pr[-1] * abi + pi[-1] * abr]
    pwr, pwi = jnp.stack(pr), jnp.stack(pi)

    def block_diag(blocks):
        n, _, r, c = blocks.shape
        tiled = jnp.tile(blocks.reshape(n, SSM_GROUPS * r, c), (1, 1, SSM_GROUPS))
        row_group = lax.broadcasted_iota(jnp.int32, tiled.shape[1:], 0) // r
        col_group = lax.broadcasted_iota(jnp.int32, tiled.shape[1:], 1) // c
        return jnp.where((row_group == col_group)[None], tiled, 0.0)

    abr_t = pwr[:SSM_SUB, :, :, None] * bbr[None] - pwi[:SSM_SUB, :, :, None] * bbi[None]
    abi_t = pwr[:SSM_SUB, :, :, None] * bbi[None] + pwi[:SSM_SUB, :, :, None] * bbr[None]
    kt = (jnp.einsum('gop,tgpc->tgco', c_re, abr_t, precision=HI)
          - jnp.einsum('gop,tgpc->tgco', c_im, abi_t, precision=HI))
    kbd = block_diag(kt)
    rev_r = jnp.stack([abr_t[SSM_SUB - 1 - s] for s in range(SSM_SUB)])
    rev_i = jnp.stack([abi_t[SSM_SUB - 1 - s] for s in range(SSM_SUB)])
    bm = jnp.concatenate([block_diag(rev_r.transpose(0, 1, 3, 2)), block_diag(rev_i.transpose(0, 1, 3, 2))], axis=-1)
    ar1, ai1 = pwr[1:, :, None, :], pwi[1:, :, None, :]
    mr = c_re[None] * ar1 - c_im[None] * ai1
    mi = c_re[None] * ai1 + c_im[None] * ar1
    cm = jnp.concatenate([block_diag(mr.transpose(0, 1, 3, 2)), block_diag(-mi.transpose(0, 1, 3, 2))], axis=1)
    a8 = jnp.concatenate([pwr[SSM_SUB].reshape(1, -1), pwi[SSM_SUB].reshape(1, -1)], axis=1)
    return kbd.astype(BF16), bm.astype(BF16), cm.astype(BF16), a8, d_skip.reshape(1, SSM_WIDTH)


def _conv_kernel(cur_ref, halo_ref, wdw_ref, bdw_ref, lng_ref, lnb_ref, wpw_ref, o_ref, pad_scr, *, rb):
    t = pl.program_id(1)

    def glu(blk):
        return blk[:, :CONV_WIDTH] * jax.nn.sigmoid(blk[:, CONV_WIDTH:])

    pad_scr[0:CONV_HALO, :] = glu(halo_ref[...]) * (t > 0).astype(F32)
    pad_scr[CONV_HALO:CONV_HALO + rb, :] = glu(cur_ref[...])
    pad_scr[CONV_HALO + rb:, :] = jnp.zeros((8, CONV_WIDTH), F32)
    y = jnp.zeros((rb, CONV_WIDTH), F32) + bdw_ref[...]
    for phase in range(8):
        part = None
        for tap in range(CONV_KSIZE):
            off = CONV_HALO - (CONV_KSIZE - 1) + tap
            if off % 8 != phase:
                continue
            term = pad_scr[off - phase:off - phase + rb + 8, :] * wdw_ref[tap:tap + 1, :]
            part = term if part is None else part + term
        y = y + part[phase:phase + rb, :]
    mu = jnp.mean(y, axis=-1, keepdims=True)
    var = jnp.mean(jnp.square(y - mu), axis=-1, keepdims=True)
    z = (y - mu) * lax.rsqrt(var + NORM_EPS) * lng_ref[...] + lnb_ref[...]
    z = (z * jax.nn.sigmoid(z)).astype(BF16)
    o_ref[...] = jnp.dot(z, wpw_ref[...], preferred_element_type=F32)


def _conv_call(uc, w_dw, b_dw, ln_g, ln_b, w_pw, *, rb):
    bsz, seq, _ = uc.shape
    assert seq % rb == 0 and rb % CONV_HALO == 0
    per = rb // CONV_HALO
    return pl.pallas_call(
        functools.partial(_conv_kernel, rb=rb),
        grid=(bsz, seq // rb),
        in_specs=[
            pl.BlockSpec((None, rb, 2 * CONV_WIDTH), lambda b, t: (b, t, 0)),
            pl.BlockSpec((None, CONV_HALO, 2 * CONV_WIDTH), lambda b, t: (b, jnp.maximum(t * per - 1, 0), 0)),
            _resident((CONV_KSIZE, CONV_WIDTH)),
            _resident((1, CONV_WIDTH)),
            _resident((1, CONV_WIDTH)),
            _resident((1, CONV_WIDTH)),
            _resident((CONV_WIDTH, CONV_WIDTH)),
        ],
        out_specs=pl.BlockSpec((None, rb, CONV_WIDTH), lambda b, t: (b, t, 0)),
        out_shape=jax.ShapeDtypeStruct((bsz, seq, CONV_WIDTH), F32),
        scratch_shapes=[pltpu.VMEM((rb + CONV_HALO + 8, CONV_WIDTH), F32)],
        compiler_params=_cparams(("arbitrary", "arbitrary")),
        name="conformer_conv",
    )(uc, uc, w_dw, b_dw, ln_g, ln_b, w_pw)


def _reorder_w_in(w_in):
    splits = [0]
    for c in IN_COLS:
        splits.append(splits[-1] + c)
    q, k, v, qi, ki, wi, us, uc = [w_in[:, splits[i]:splits[i + 1]] for i in range(len(IN_COLS))]
    pad = jnp.zeros((w_in.shape[0], PROJ_SSM0 - PROJ_KW0 - IDX_DIM - IDX_HEADS), w_in.dtype)
    return jnp.concatenate([q, k, v, qi, ki, wi, pad, us, uc], axis=1).astype(BF16)


def kernel(x, c, mod_w, mod_b, ffn1_norm, ffn1_w_gu, ffn1_w_down, mix_norm, w_in, w_out, att_out_norm, ssm_out_norm, conv_out_norm, ssm_a_re, ssm_a_im, ssm_log_dt, ssm_b_re, ssm_b_im, ssm_c_re, ssm_c_im, ssm_d, ssm_w_glu, conv_w_dw, conv_b_dw, conv_ln_g, conv_ln_b, conv_w_pw, ffn2_norm, ffn2_w_gu, ffn2_w_down, final_norm):
    bsz, seq, _ = x.shape
    depth = mod_w.shape[0]
    tm = min(512, seq)
    tq = min(256, seq)
    mod = _mod_call(c, mod_w, mod_b).reshape(depth, bsz, N_MOD, D_MODEL)
    row = lambda v: v.reshape(1, -1)
    fin = row(final_norm)
    for l in range(depth):
        mod3 = mod[l]
        wi0 = sum(IN_COLS[:5])
        w_wi_t = w_in[l][:, wi0:wi0 + IDX_HEADS].T.astype(BF16)
        x, q, k, v, qi, ki, wi, us, uc = _ffn_proj_call(
            x, mod3, row(ffn1_norm[l]), ffn1_w_gu[l].astype(BF16), ffn1_w_down[l].astype(BF16), fin,
            row(mix_norm[l]), _reorder_w_in(w_in[l]), w_wi_t, tm=tm)
        y_att = _att_call(q, k, v, qi, ki, wi, tq=tq, kb=256)
        y_ssm = _ssm_call(us, *_ssm_params(ssm_a_re[l], ssm_a_im[l], ssm_log_dt[l], ssm_b_re[l], ssm_b_im[l],
                                           ssm_c_re[l], ssm_c_im[l], ssm_d[l]), ssm_w_glu[l].astype(BF16))
        y_conv = _conv_call(uc, conv_w_dw[l], row(conv_b_dw[l]), row(conv_ln_g[l]), row(conv_ln_b[l]),
                            conv_w_pw[l].astype(BF16), rb=min(256, seq))
        x = _mix_ffn_call(x, mod3, y_att, y_ssm, y_conv, row(att_out_norm[l]), row(ssm_out_norm[l]),
                          row(conv_out_norm[l]), w_out[l].astype(BF16), row(ffn2_norm[l]),
                          ffn2_w_gu[l].astype(BF16), ffn2_w_down[l].astype(BF16), fin,
                          final=(l == depth - 1), tm=tm)
    return x
```

```python
import functools
import math

import jax
import jax.numpy as jnp
from jax import lax
from jax.experimental import pallas as pl
from jax.experimental.pallas import tpu as pltpu

F32 = jnp.float32
BF16 = jnp.bfloat16

D_MODEL = 1024
CHUNK = 64
ATT_HEADS = 8
ATT_HEAD_DIM = 64
ATT_WIDTH = ATT_HEADS * ATT_HEAD_DIM
IDX_HEADS = 8
IDX_DIM = 32
TOPK_MAX = 256
TOPK_FRACTION = 4
SSM_WIDTH = 256
SSM_GROUP = 16
SSM_GROUPS = 16
SSM_STATE = 64
SSM_LANES = 2 * SSM_GROUPS * SSM_STATE
SSM_SUB = 8
CONV_WIDTH = 256
CONV_KSIZE = 31
CONV_HALO = 32
D_FF = 2816
N_MOD = 9
NORM_EPS = 1e-6
IN_COLS = (ATT_WIDTH, ATT_HEAD_DIM, ATT_HEAD_DIM, IDX_HEADS * IDX_DIM, IDX_DIM, IDX_HEADS, SSM_WIDTH, 2 * CONV_WIDTH)

VMEM_LIMIT = 56 * 1024 * 1024
NEG_BIG = -1e30
LOG2_E = 1.4426950408889634
KEY_NEG_INF = -2139095041
KEY_POS_INF = 2139095040
HI = lax.Precision.HIGHEST


def _cparams(sem, fuse_inputs=None):
    return pltpu.CompilerParams(dimension_semantics=sem, vmem_limit_bytes=VMEM_LIMIT, allow_input_fusion=fuse_inputs)


def _resident(shape):
    nd = len(shape)
    return pl.BlockSpec(shape, lambda *_: (0,) * nd, pipeline_mode=pl.Buffered(1))


def _nt_dot(a, b):
    return lax.dot_general(a, b, (((1,), (1,)), ((), ())), preferred_element_type=F32)


def _modulated_norm(x, gain, shift, scale):
    y = x * lax.rsqrt(jnp.mean(x * x, axis=-1, keepdims=True) + NORM_EPS) * gain
    return y * (1.0 + scale) + shift


def _mod_kernel(c_ref, w_ref, b_ref, o_ref):
    c = c_ref[...]
    cond = c * jax.nn.sigmoid(c)
    o_ref[...] = jnp.dot(cond, w_ref[...], preferred_element_type=F32, precision=HI) + b_ref[...]


def _mod_call(c, mod_w, mod_b):
    depth = mod_w.shape[0]
    bsz = c.shape[0]
    return pl.pallas_call(
        _mod_kernel,
        grid=(depth, N_MOD),
        in_specs=[
            pl.BlockSpec((bsz, D_MODEL), lambda l, j: (0, 0)),
            pl.BlockSpec((None, D_MODEL, D_MODEL), lambda l, j: (l, 0, j)),
            pl.BlockSpec((None, 1, D_MODEL), lambda l, j: (l, 0, j)),
        ],
        out_specs=pl.BlockSpec((None, bsz, D_MODEL), lambda l, j: (l, 0, j)),
        out_shape=jax.ShapeDtypeStruct((depth, bsz, N_MOD * D_MODEL), F32),
        compiler_params=_cparams(("arbitrary", "arbitrary")),
        name="adaln_mod",
    )(c, mod_w, mod_b.reshape(depth, 1, N_MOD * D_MODEL))


FFN_CHUNK = 256


def _rms_gain(y, gain):
    return (y * lax.rsqrt(jnp.mean(y * y, axis=-1, keepdims=True) + NORM_EPS) * gain).astype(BF16)


def _mix_ffn_kernel(x_ref, mod_ref, ya_ref, ys_ref, yc_ref, ga_ref, gs_ref, gc_ref, wo_ref,
                    ng_ref, wgu_ref, wd_ref, fn_ref, o_ref, *, final):
    mix = jnp.dot(_rms_gain(ya_ref[...], ga_ref[...]), wo_ref[0:ATT_WIDTH, :], preferred_element_type=F32)
    mix = mix + jnp.dot(_rms_gain(ys_ref[...], gs_ref[...]), wo_ref[ATT_WIDTH:ATT_WIDTH + SSM_WIDTH, :],
                        preferred_element_type=F32)
    mix = mix + jnp.dot(_rms_gain(yc_ref[...], gc_ref[...]), wo_ref[ATT_WIDTH + SSM_WIDTH:, :],
                        preferred_element_type=F32)
    x = x_ref[...] + mod_ref[5:6, :] * mix
    o_ref[...] = _ffn_math(x, mod_ref, ng_ref, wgu_ref, wd_ref, fn_ref, row0=6, final=final)


def _ffn_math(x, mod_ref, ng_ref, wgu_ref, wd_ref, fn_ref, *, row0, final):
    h = _modulated_norm(x, ng_ref[...], mod_ref[row0:row0 + 1, :], mod_ref[row0 + 1:row0 + 2, :]).astype(BF16)
    acc = jnp.zeros(x.shape, F32)
    for c0 in range(0, D_FF, FFN_CHUNK):
        gate = jnp.dot(h, wgu_ref[:, c0:c0 + FFN_CHUNK], preferred_element_type=F32)
        up = jnp.dot(h, wgu_ref[:, D_FF + c0:D_FF + c0 + FFN_CHUNK], preferred_element_type=F32)
        a = (gate * jax.nn.sigmoid(gate) * up).astype(BF16)
        acc = acc + jnp.dot(a, wd_ref[c0:c0 + FFN_CHUNK, :], preferred_element_type=F32)
    out = x + (0.5 * mod_ref[row0 + 2:row0 + 3, :]) * acc
    if final:
        out = out * lax.rsqrt(jnp.mean(out * out, axis=-1, keepdims=True) + NORM_EPS) * fn_ref[...]
    return out


def _mix_ffn_call(x, mod3, y_att, y_ssm, y_conv, g_att, g_ssm, g_conv, w_out, norm_g, w_gu, w_down, final_g,
                  *, final, tm):
    bsz, seq, _ = x.shape
    tok = lambda width: pl.BlockSpec((None, tm, width), lambda b, t: (b, t, 0))
    return pl.pallas_call(
        functools.partial(_mix_ffn_kernel, final=final),
        grid=(bsz, seq // tm),
        in_specs=[
            tok(D_MODEL),
            pl.BlockSpec((None, N_MOD, D_MODEL), lambda b, t: (b, 0, 0)),
            tok(ATT_WIDTH), tok(SSM_WIDTH), tok(CONV_WIDTH),
            _resident((1, ATT_WIDTH)), _resident((1, SSM_WIDTH)), _resident((1, CONV_WIDTH)),
            _resident((D_MODEL, D_MODEL)),
            _resident((1, D_MODEL)),
            _resident((D_MODEL, 2 * D_FF)),
            _resident((D_FF, D_MODEL)),
            _resident((1, D_MODEL)),
        ],
        out_specs=tok(D_MODEL),
        out_shape=jax.ShapeDtypeStruct(x.shape, F32),
        compiler_params=_cparams(("arbitrary", "arbitrary"), [i in (8, 10, 11) for i in range(13)]),
        name="mix_ffn_final" if final else "mix_ffn",
    )(x, mod3, y_att, y_ssm, y_conv, g_att, g_ssm, g_conv, w_out, norm_g, w_gu, w_down, final_g)


PROJ_Q0, PROJ_KV0, PROJ_QI0, PROJ_KW0, PROJ_SSM0, PROJ_CONV0, PROJ_END = 0, 512, 640, 896, 1024, 1280, 1792


def _ffn_proj_kernel(x_ref, mod_ref, fng_ref, wgu_ref, wd_ref, fn_ref, ng_ref, w_ref, wit_ref,
                     xo_ref, q_ref, k_ref, v_ref, qi_ref, ki_ref, wi_ref, us_ref, uc_ref):
    x = _ffn_math(x_ref[...], mod_ref, fng_ref, wgu_ref, wd_ref, fn_ref, row0=0, final=False)
    xo_ref[...] = x
    h = _modulated_norm(x, ng_ref[...], mod_ref[3:4, :], mod_ref[4:5, :]).astype(BF16)
    q = jnp.dot(h, w_ref[:, PROJ_Q0:PROJ_KV0], preferred_element_type=F32) * (ATT_HEAD_DIM ** -0.5 * LOG2_E)
    for hd in range(ATT_HEADS):
        q_ref[hd] = q[:, hd * ATT_HEAD_DIM:(hd + 1) * ATT_HEAD_DIM].astype(BF16)
    kv = jnp.dot(h, w_ref[:, PROJ_KV0:PROJ_QI0], preferred_element_type=F32)
    k_ref[...] = kv[:, :ATT_HEAD_DIM].astype(BF16)
    lane = lax.broadcasted_iota(jnp.int32, kv.shape, 1)
    ones_col = jnp.where(lane == ATT_HEAD_DIM, 1.0, 0.0)
    v_ref[...] = jnp.where(lane < ATT_HEAD_DIM, pltpu.roll(kv, ATT_HEAD_DIM, 1), ones_col).astype(BF16)
    qi = jnp.dot(h, w_ref[:, PROJ_QI0:PROJ_KW0], preferred_element_type=F32)
    for hd in range(IDX_HEADS):
        qi_ref[hd] = qi[:, hd * IDX_DIM:(hd + 1) * IDX_DIM].astype(BF16)
    kw = jnp.dot(h, w_ref[:, PROJ_KW0:PROJ_SSM0], preferred_element_type=F32)
    ki_ref[...] = kw[:, :IDX_DIM].astype(BF16)
    wi_ref[...] = _nt_dot(wit_ref[...], h) * (IDX_DIM ** -0.5 * IDX_HEADS ** -0.5)
    us_ref[...] = jnp.dot(h, w_ref[:, PROJ_SSM0:PROJ_CONV0], preferred_element_type=F32)
    uc_ref[...] = jnp.dot(h, w_ref[:, PROJ_CONV0:PROJ_END], preferred_element_type=F32)


def _ffn_proj_call(x, mod3, ffn_norm_g, w_gu, w_down, final_g, norm_g, w_in_r, w_wi_t, *, tm):
    bsz, seq, _ = x.shape
    nt = seq // tm
    tok = lambda width: pl.BlockSpec((None, tm, width), lambda b, t: (b, t, 0))
    head = lambda n, width: pl.BlockSpec((None, n, tm, width), lambda b, t: (b, 0, t, 0))
    return pl.pallas_call(
        _ffn_proj_kernel,
        grid=(bsz, nt),
        in_specs=[
            tok(D_MODEL),
            pl.BlockSpec((None, N_MOD, D_MODEL), lambda b, t: (b, 0, 0)),
            _resident((1, D_MODEL)),
            _resident((D_MODEL, 2 * D_FF)),
            _resident((D_FF, D_MODEL)),
            _resident((1, D_MODEL)),
            _resident((1, D_MODEL)),
            _resident((D_MODEL, PROJ_END)),
            _resident((IDX_HEADS, D_MODEL)),
        ],
        out_specs=[
            tok(D_MODEL),
            head(ATT_HEADS, ATT_HEAD_DIM), tok(ATT_HEAD_DIM), tok(2 * ATT_HEAD_DIM),
            head(IDX_HEADS, IDX_DIM), tok(IDX_DIM),
            pl.BlockSpec((None, IDX_HEADS, tm), lambda b, t: (b, 0, t)),
            tok(SSM_WIDTH), tok(2 * CONV_WIDTH),
        ],
        out_shape=[
            jax.ShapeDtypeStruct(x.shape, F32),
            jax.ShapeDtypeStruct((bsz, ATT_HEADS, seq, ATT_HEAD_DIM), BF16),
            jax.ShapeDtypeStruct((bsz, seq, ATT_HEAD_DIM), BF16),
            jax.ShapeDtypeStruct((bsz, seq, 2 * ATT_HEAD_DIM), BF16),
            jax.ShapeDtypeStruct((bsz, IDX_HEADS, seq, IDX_DIM), BF16),
            jax.ShapeDtypeStruct((bsz, seq, IDX_DIM), BF16),
            jax.ShapeDtypeStruct((bsz, IDX_HEADS, seq), F32),
            jax.ShapeDtypeStruct((bsz, seq, SSM_WIDTH), F32),
            jax.ShapeDtypeStruct((bsz, seq, 2 * CONV_WIDTH), F32),
        ],
        compiler_params=_cparams(("arbitrary", "arbitrary"), [i in (3, 4) for i in range(9)]),
        name="ffn_in_proj",
    )(x, mod3, ffn_norm_g, w_gu, w_down, final_g, norm_g, w_in_r, w_wi_t)


def _key_to_f32(key):
    return lax.bitcast_convert_type(jnp.where(key >= 0, key, key ^ 0x7FFFFFFF), F32)


def _att_kernel(q_ref, k_ref, v_ref, qi_ref, ki_ref, wi_ref, o_ref,
                idx_scr, xb_scr, lg_scr, m_scr, acc_scr, j_scr, thr_scr, cnt_scr, *, tq, kb, topk, n_tie_iter, nq):
    j = pl.program_id(1)
    nkb = (j * tq + tq + kb - 1) // kb
    heads = ATT_HEADS
    rows = heads * tq
    half = kb // 2

    qcol = lax.broadcasted_iota(jnp.int32, (1, tq), 1)
    limit = j * tq + ((qcol // CHUNK) + 1) * CHUNK
    key0 = lax.broadcasted_iota(jnp.int32, (kb, tq), 0)
    wt = wi_ref[...]

    def for_key_blocks(body):
        n4 = nkb // 4

        def quad(i, carry):
            for u in range(4):
                body(i * 4 + u)
            return carry

        lax.fori_loop(0, n4, quad, 0)
        rem = nkb - n4 * 4

        @pl.when((rem & 2) != 0)
        def _():
            body(n4 * 4)
            body(n4 * 4 + 1)

        @pl.when((rem & 1) != 0)
        def _():
            body(n4 * 4 + (rem & 2))

    def idx_body(b):
        kib = ki_ref[pl.ds(pl.multiple_of(b * kb, kb), kb), :]
        acc = jnp.zeros((kb, tq), F32)
        for hd in range(IDX_HEADS):
            acc = acc + jnp.maximum(_nt_dot(kib, qi_ref[hd]), 0.0) * wt[hd:hd + 1, :]
        score = jnp.where(key0 + b * kb < limit, acc, -jnp.inf)
        idx_scr[b] = score
        xb_scr[b] = score.astype(BF16)

    for_key_blocks(idx_body)

    def over_blocks(nb, body, init):
        if isinstance(nb, int):
            acc = init
            for b in range(nb):
                acc = body(b, acc)
            return acc
        return lax.fori_loop(0, nb, body, init)

    def reduce_keys(fn, init, combine, nb=nkb):
        def body(b, acc):
            return combine(acc, fn(idx_scr[b], b).reshape(kb // 8, 8, tq))
        return over_blocks(nb, body, jnp.full((8, tq), init, F32))

    def count(pred, nb=nkb):
        acc = reduce_keys(lambda blk, b: jnp.where(pred(blk, b), 1.0, 0.0), 0.0,
                          lambda a, m: a + jnp.sum(m, axis=0), nb)
        return jnp.sum(acc, axis=0, keepdims=True)

    one_b, zero_b = jnp.ones((16, tq), BF16), jnp.zeros((16, tq), BF16)

    def count16(cand_b, nb):
        def body(b, acc):
            blk = xb_scr[b]
            for i in range(kb // 16):
                acc = acc + jnp.where(blk[16 * i:16 * (i + 1)] >= cand_b, one_b, zero_b)
            return acc
        return jnp.sum(over_blocks(nb, body, zero_b).astype(F32), axis=0, keepdims=True)

    def search(nb):
        def coarse_body(_, carry):
            lo, hi = carry
            mid = (lo + hi + 1) >> 1
            pattern = jnp.where(mid >= 0, mid, mid ^ 0x7FFF) << 16
            cand = lax.bitcast_convert_type(pattern, F32).astype(BF16)
            ok = count16(jnp.broadcast_to(cand, (16, tq)), nb) >= topk
            return jnp.where(ok, mid, lo), jnp.where(ok, hi, mid - 1)

        c16, _ = lax.fori_loop(0, 16, coarse_body, (jnp.full((1, tq), KEY_NEG_INF >> 16, jnp.int32),
                                                    jnp.full((1, tq), KEY_POS_INF >> 16, jnp.int32)))
        key_t16 = jnp.where(c16 >= 0, c16 << 16, (c16 << 16) | 0xFFFF)

        def fine_body(_, carry):
            lo, hi = carry
            mid = (lo | hi) - ((lo ^ hi) >> 1)
            cand = _key_to_f32(mid)
            ok = count(lambda blk, b: blk >= cand, nb) >= topk
            return jnp.where(ok, mid, lo), jnp.where(ok, hi, mid - 1)

        lo, _ = lax.fori_loop(0, 17, fine_body, (jnp.maximum(key_t16 - 32770, KEY_NEG_INF),
                                                 jnp.minimum(key_t16 + 65536, KEY_POS_INF)))
        thr = _key_to_f32(lo)
        return thr, count(lambda blk, b: blk > thr, nb), count(lambda blk, b: blk == thr, nb)

    for jj in range(nq):
        @pl.when(j == jj)
        def _(jj=jj):
            thr_s, n_gt_s, n_eq_s = search((jj * tq + tq + kb - 1) // kb)
            thr_scr[...] = jnp.broadcast_to(thr_s, (8, tq))
            cnt_scr[0] = jnp.broadcast_to(n_gt_s, (8, tq))
            cnt_scr[1] = jnp.broadcast_to(n_eq_s, (8, tq))

    thr = thr_scr[0:1, :]
    n_gt = cnt_scr[0, 0:1, :]
    n_eq = cnt_scr[1, 0:1, :]

    def refine_cond(carry):
        return jnp.max(jnp.where(carry[1] >= topk, 1, 0)) > 0

    def refine_body(carry):
        cur, cnt, _ = carry
        above = reduce_keys(lambda blk, b: jnp.where(blk > cur, blk, jnp.inf), jnp.inf,
                            lambda a, m: jnp.minimum(a, jnp.min(m, axis=0)))
        cur = jnp.where(cnt >= topk, jnp.min(above, axis=0, keepdims=True), cur)
        return cur, count(lambda blk, b: blk > cur), count(lambda blk, b: blk == cur)

    thr, n_gt, n_eq = lax.while_loop(refine_cond, refine_body, (thr, n_gt, n_eq))
    need = topk - n_gt
    has_tie = jnp.logical_and(n_eq > need, thr > -jnp.inf)
    j_max = nkb * kb - 1
    j_scr[...] = jnp.full((8, tq), j_max, jnp.int32)

    @pl.when(jnp.max(jnp.where(has_tie, 1, 0)) > 0)
    def _():
        def tie_body(_, carry):
            tlo, thi = carry
            mid = (tlo + thi) >> 1
            ok = count(lambda blk, b: jnp.logical_and(blk == thr, key0 + b * kb <= mid)) >= need
            return jnp.where(ok, tlo, mid + 1), jnp.where(ok, mid, thi)
        tlo, _ = lax.fori_loop(0, n_tie_iter, tie_body,
                               (jnp.zeros((1, tq), jnp.int32), jnp.full((1, tq), j_max, jnp.int32)))
        j_scr[...] = jnp.broadcast_to(jnp.where(has_tie, tlo, j_max), (8, tq))

    j_sel = jnp.where(thr > -jnp.inf, j_scr[0:1, :], -1)

    qall = q_ref[...].reshape(rows, ATT_HEAD_DIM)
    m_scr[...] = jnp.full((rows, half), NEG_BIG, F32)

    def logits_body(b):
        kblk = k_ref[pl.ds(pl.multiple_of(b * kb, kb), kb), :]
        blk = idx_scr[b]
        sel = jnp.logical_or(blk > thr, jnp.logical_and(blk == thr, key0 + b * kb <= j_sel))
        bias = jnp.where(sel, 0.0, NEG_BIG).T
        lg = _nt_dot(qall, kblk).reshape(heads, tq, kb) + bias[None]
        lg_scr[b] = lg
        lg2 = lg.reshape(rows, kb)
        m_scr[...] = jnp.maximum(m_scr[...], jnp.maximum(lg2[:, :half], lg2[:, half:]))

    for_key_blocks(logits_body)
    m_scr[...] = jnp.broadcast_to(jnp.max(m_scr[...], axis=1, keepdims=True), (rows, half))

    acc_scr[...] = jnp.zeros((rows, 2 * ATT_HEAD_DIM), F32)

    def pv_body(b):
        vblk = v_ref[pl.ds(pl.multiple_of(b * kb, kb), kb), :]
        lg2 = lg_scr[b].reshape(rows, kb)
        mrow = m_scr[...]
        p = jnp.concatenate([jnp.exp2(lg2[:, :half] - mrow), jnp.exp2(lg2[:, half:] - mrow)], axis=1).astype(BF16)
        acc_scr[...] += jnp.dot(p, vblk, preferred_element_type=F32)

    for_key_blocks(pv_body)
    acc = acc_scr[...]
    out = acc[:, :ATT_HEAD_DIM] / acc[:, ATT_HEAD_DIM:ATT_HEAD_DIM + 1]
    for hd in range(heads):
        o_ref[:, hd * ATT_HEAD_DIM:(hd + 1) * ATT_HEAD_DIM] = out[hd * tq:(hd + 1) * tq]


def _att_call(q, k, v, qi, ki, wi, *, tq, kb):
    bsz, _, seq, _ = q.shape
    topk = min(TOPK_MAX, seq // TOPK_FRACTION)
    assert tq % CHUNK == 0 and tq >= topk and seq % tq == 0 and seq % kb == 0 and kb % 256 == 0
    nq = seq // tq
    nkb_max = seq // kb
    assert (kb // 16) * nkb_max <= 256
    rows = ATT_HEADS * tq
    kern = functools.partial(_att_kernel, tq=tq, kb=kb, topk=topk, n_tie_iter=max(1, math.ceil(math.log2(seq))),
                             nq=nq)
    return pl.pallas_call(
        kern,
        grid=(bsz, nq),
        in_specs=[
            pl.BlockSpec((None, ATT_HEADS, tq, ATT_HEAD_DIM), lambda b, j: (b, 0, j, 0)),
            pl.BlockSpec((None, seq, ATT_HEAD_DIM), lambda b, j: (b, 0, 0)),
            pl.BlockSpec((None, seq, 2 * ATT_HEAD_DIM), lambda b, j: (b, 0, 0)),
            pl.BlockSpec((None, IDX_HEADS, tq, IDX_DIM), lambda b, j: (b, 0, j, 0)),
            pl.BlockSpec((None, seq, IDX_DIM), lambda b, j: (b, 0, 0)),
            pl.BlockSpec((None, IDX_HEADS, tq), lambda b, j: (b, 0, j)),
        ],
        out_specs=pl.BlockSpec((None, tq, ATT_WIDTH), lambda b, j: (b, j, 0)),
        out_shape=jax.ShapeDtypeStruct((bsz, seq, ATT_WIDTH), F32),
        scratch_shapes=[
            pltpu.VMEM((nkb_max, kb, tq), F32),
            pltpu.VMEM((nkb_max, kb, tq), BF16),
            pltpu.VMEM((nkb_max, ATT_HEADS, tq, kb), F32),
            pltpu.VMEM((rows, kb // 2), F32),
            pltpu.VMEM((rows, 2 * ATT_HEAD_DIM), F32),
            pltpu.VMEM((8, tq), jnp.int32),
            pltpu.VMEM((8, tq), F32),
            pltpu.VMEM((2, 8, tq), F32),
        ],
        compiler_params=_cparams(("arbitrary", "arbitrary")),
        name="dsa_attention",
    )(q, k, v, qi, ki, wi)


def _ssm_kernel(u_ref, kbd_ref, bm_ref, cm_ref, a8_ref, d_ref, wglu_ref, o_ref, st_scr, y_scr, u_scr, *, seq):
    nc = seq // SSM_SUB
    half = SSM_LANES // 2
    u = u_ref[...]
    sub = lax.broadcasted_iota(jnp.int32, (seq, 1), 0) % SSM_SUB
    y = u * d_ref[...] + jnp.dot(u.astype(BF16), kbd_ref[0], preferred_element_type=F32)
    for tau in range(1, SSM_SUB):
        shifted = jnp.where(sub >= tau, pltpu.roll(u, tau, 0), 0.0).astype(BF16)
        y = y + jnp.dot(shifted, kbd_ref[tau], preferred_element_type=F32)
    for hf in range(2):
        y_scr[hf] = y[:, hf * 128:(hf + 1) * 128]
        u_scr[hf] = u[:, hf * 128:(hf + 1) * 128]

    e = jnp.zeros((nc, SSM_LANES), F32)
    for s in range(SSM_SUB):
        rows = pl.ds(s, nc, stride=SSM_SUB)
        us = jnp.concatenate([u_scr[0, rows, :], u_scr[1, rows, :]], axis=1).astype(BF16)
        e = e + jnp.dot(us, bm_ref[s], preferred_element_type=F32)
    st_scr[...] = e

    a8r = a8_ref[:, :half]
    a8i = a8_ref[:, half:]

    def scan_body(c, carry):
        xr, xi = carry
        inc = st_scr[pl.ds(c, 1), :]
        st_scr[pl.ds(c, 1), :] = jnp.concatenate([xr, xi], axis=1)
        return (a8r * xr - a8i * xi + inc[:, :half], a8r * xi + a8i * xr + inc[:, half:])

    lax.fori_loop(0, nc, scan_body, (jnp.zeros((1, half), F32), jnp.zeros((1, half), F32)))

    x_in = st_scr[...].astype(BF16)
    for i in range(SSM_SUB):
        rows = pl.ds(i, nc, stride=SSM_SUB)
        yi = jnp.dot(x_in, cm_ref[i], preferred_element_type=F32)
        for hf in range(2):
            y_scr[hf, rows, :] = y_scr[hf, rows, :] + yi[:, hf * 128:(hf + 1) * 128]

    act = jax.nn.gelu(jnp.concatenate([y_scr[0], y_scr[1]], axis=1)).astype(BF16)
    ag = jnp.dot(act, wglu_ref[...], preferred_element_type=F32)
    o_ref[...] = ag[:, :SSM_WIDTH] * jax.nn.sigmoid(ag[:, SSM_WIDTH:])


def _ssm_call(u, kbd, bm, cm, a8, d, w_glu):
    bsz, seq, _ = u.shape
    assert seq % SSM_SUB == 0
    return pl.pallas_call(
        functools.partial(_ssm_kernel, seq=seq),
        grid=(bsz,),
        in_specs=[
            pl.BlockSpec((None, seq, SSM_WIDTH), lambda b: (b, 0, 0)),
            _resident((SSM_SUB, SSM_WIDTH, SSM_WIDTH)),
            _resident((SSM_SUB, SSM_WIDTH, SSM_LANES)),
            _resident((SSM_SUB, SSM_LANES, SSM_WIDTH)),
            _resident((1, SSM_LANES)),
            _resident((1, SSM_WIDTH)),
            _resident((SSM_WIDTH, 2 * SSM_WIDTH)),
        ],
        out_specs=pl.BlockSpec((None, seq, SSM_WIDTH), lambda b: (b, 0, 0)),
        out_shape=jax.ShapeDtypeStruct((bsz, seq, SSM_WIDTH), F32),
        scratch_shapes=[pltpu.VMEM((seq // SSM_SUB, SSM_LANES), F32), pltpu.VMEM((2, seq, 128), F32),
                        pltpu.VMEM((2, seq, 128), F32)],
        compiler_params=_cparams(("arbitrary",)),
        name="s5_ssm",
    )(u, kbd, bm, cm, a8, d, w_glu)


def _ssm_params(a_re, a_im, log_dt, b_re, b_im, c_re, c_im, d_skip):
    dt = jnp.exp(log_dt)[:, None]
    mag = jnp.exp(dt * a_re)
    abr, abi = mag * jnp.cos(dt * a_im), mag * jnp.sin(dt * a_im)
    den = a_re * a_re + a_im * a_im
    nr, ni = abr - 1.0, abi
    qr, qi = (nr * a_re + ni * a_im) / den, (ni * a_re - nr * a_im) / den
    bbr = qr[..., None] * b_re - qi[..., None] * b_im
    bbi = qr[..., None] * b_im + qi[..., None] * b_re
    pr, pi = [jnp.ones_like(abr)], [jnp.zeros_like(abr)]
    for _ in range(SSM_SUB):
        pr, pi = pr + [pr[-1] * abr - pi[-1] * abi], pi + [pr[-1] * abi + pi[-1] * abr]
    pwr, pwi = jnp.stack(pr), jnp.stack(pi)

    def block_diag(blocks):
        n, _, r, c = blocks.shape
        tiled = jnp.tile(blocks.reshape(n, SSM_GROUPS * r, c), (1, 1, SSM_GROUPS))
        row_group = lax.broadcasted_iota(jnp.int32, tiled.shape[1:], 0) // r
        col_group = lax.broadcasted_iota(jnp.int32, tiled.shape[1:], 1) // c
        return jnp.where((row_group == col_group)[None], tiled, 0.0)

    abr_t = pwr[:SSM_SUB, :, :, None] * bbr[None] - pwi[:SSM_SUB, :, :, None] * bbi[None]
    abi_t = pwr[:SSM_SUB, :, :, None] * bbi[None] + pwi[:SSM_SUB, :, :, None] * bbr[None]
    kt = (jnp.einsum('gop,tgpc->tgco', c_re, abr_t, precision=HI)
          - jnp.einsum('gop,tgpc->tgco', c_im, abi_t, precision=HI))
    kbd = block_diag(kt)
    rev_r = jnp.stack([abr_t[SSM_SUB - 1 - s] for s in range(SSM_SUB)])
    rev_i = jnp.stack([abi_t[SSM_SUB - 1 - s] for s in range(SSM_SUB)])
    bm = jnp.concatenate([block_diag(rev_r.transpose(0, 1, 3, 2)), block_diag(rev_i.transpose(0, 1, 3, 2))], axis=-1)
    ar1, ai1 = pwr[1:, :, None, :], pwi[1:, :, None, :]
    mr = c_re[None] * ar1 - c_im[None] * ai1
    mi = c_re[None] * ai1 + c_im[None] * ar1
    cm = jnp.concatenate([block_diag(mr.transpose(0, 1, 3, 2)), block_diag(-mi.transpose(0, 1, 3, 2))], axis=1)
    a8 = jnp.concatenate([pwr[SSM_SUB].reshape(1, -1), pwi[SSM_SUB].reshape(1, -1)], axis=1)
    return kbd.astype(BF16), bm.astype(BF16), cm.astype(BF16), a8, d_skip.reshape(1, SSM_WIDTH)


def _conv_kernel(cur_ref, halo_ref, wdw_ref, bdw_ref, lng_ref, lnb_ref, wpw_ref, o_ref, pad_scr, *, rb):
    t = pl.program_id(1)

    def glu(blk):
        return blk[:, :CONV_WIDTH] * jax.nn.sigmoid(blk[:, CONV_WIDTH:])

    pad_scr[0:CONV_HALO, :] = glu(halo_ref[...]) * (t > 0).astype(F32)
    pad_scr[CONV_HALO:CONV_HALO + rb, :] = glu(cur_ref[...])
    pad_scr[CONV_HALO + rb:, :] = jnp.zeros((8, CONV_WIDTH), F32)
    y = jnp.zeros((rb, CONV_WIDTH), F32) + bdw_ref[...]
    for phase in range(8):
        part = None
        for tap in range(CONV_KSIZE):
            off = CONV_HALO - (CONV_KSIZE - 1) + tap
            if off % 8 != phase:
                continue
            term = pad_scr[off - phase:off - phase + rb + 8, :] * wdw_ref[tap:tap + 1, :]
            part = term if part is None else part + term
        y = y + part[phase:phase + rb, :]
    mu = jnp.mean(y, axis=-1, keepdims=True)
    var = jnp.mean(jnp.square(y - mu), axis=-1, keepdims=True)
    z = (y - mu) * lax.rsqrt(var + NORM_EPS) * lng_ref[...] + lnb_ref[...]
    z = (z * jax.nn.sigmoid(z)).astype(BF16)
    o_ref[...] = jnp.dot(z, wpw_ref[...], preferred_element_type=F32)


def _conv_call(uc, w_dw, b_dw, ln_g, ln_b, w_pw, *, rb):
    bsz, seq, _ = uc.shape
    assert seq % rb == 0 and rb % CONV_HALO == 0
    per = rb // CONV_HALO
    return pl.pallas_call(
        functools.partial(_conv_kernel, rb=rb),
        grid=(bsz, seq // rb),
        in_specs=[
            pl.BlockSpec((None, rb, 2 * CONV_WIDTH), lambda b, t: (b, t, 0)),
            pl.BlockSpec((None, CONV_HALO, 2 * CONV_WIDTH), lambda b, t: (b, jnp.maximum(t * per - 1, 0), 0)),
            _resident((CONV_KSIZE, CONV_WIDTH)),
            _resident((1, CONV_WIDTH)),
            _resident((1, CONV_WIDTH)),
            _resident((1, CONV_WIDTH)),
            _resident((CONV_WIDTH, CONV_WIDTH)),
        ],
        out_specs=pl.BlockSpec((None, rb, CONV_WIDTH), lambda b, t: (b, t, 0)),
        out_shape=jax.ShapeDtypeStruct((bsz, seq, CONV_WIDTH), F32),
        scratch_shapes=[pltpu.VMEM((rb + CONV_HALO + 8, CONV_WIDTH), F32)],
        compiler_params=_cparams(("arbitrary", "arbitrary")),
        name="conformer_conv",
    )(uc, uc, w_dw, b_dw, ln_g, ln_b, w_pw)


def _reorder_w_in(w_in):
    splits = [0]
    for c in IN_COLS:
        splits.append(splits[-1] + c)
    q, k, v, qi, ki, wi, us, uc = [w_in[:, splits[i]:splits[i + 1]] for i in range(len(IN_COLS))]
    pad = jnp.zeros((w_in.shape[0], PROJ_SSM0 - PROJ_KW0 - IDX_DIM - IDX_HEADS), w_in.dtype)
    return jnp.concatenate([q, k, v, qi, ki, wi, pad, us, uc], axis=1).astype(BF16)


def kernel(x, c, mod_w, mod_b, ffn1_norm, ffn1_w_gu, ffn1_w_down, mix_norm, w_in, w_out, att_out_norm, ssm_out_norm, conv_out_norm, ssm_a_re, ssm_a_im, ssm_log_dt, ssm_b_re, ssm_b_im, ssm_c_re, ssm_c_im, ssm_d, ssm_w_glu, conv_w_dw, conv_b_dw, conv_ln_g, conv_ln_b, conv_w_pw, ffn2_norm, ffn2_w_gu, ffn2_w_down, final_norm):
    bsz, seq, _ = x.shape
    depth = mod_w.shape[0]
    tm = min(512, seq)
    tq = min(256, seq)
    mod = _mod_call(c, mod_w, mod_b).reshape(depth, bsz, N_MOD, D_MODEL)
    row = lambda v: v.reshape(1, -1)
    fin = row(final_norm)
    for l in range(depth):
        mod3 = mod[l]
        wi0 = sum(IN_COLS[:5])
        w_wi_t = w_in[l][:, wi0:wi0 + IDX_HEADS].T.astype(BF16)
        x, q, k, v, qi, ki, wi, us, uc = _ffn_proj_call(
            x, mod3, row(ffn1_norm[l]), ffn1_w_gu[l].astype(BF16), ffn1_w_down[l].astype(BF16), fin,
            row(mix_norm[l]), _reorder_w_in(w_in[l]), w_wi_t, tm=tm)
        y_att = _att_call(q, k, v, qi, ki, wi, tq=tq, kb=256)
        y_ssm = _ssm_call(us, *_ssm_params(ssm_a_re[l], ssm_a_im[l], ssm_log_dt[l], ssm_b_re[l], ssm_b_im[l],
                                           ssm_c_re[l], ssm_c_im[l], ssm_d[l]), ssm_w_glu[l].astype(BF16))
        y_conv = _conv_call(uc, conv_w_dw[l], row(conv_b_dw[l]), row(conv_ln_g[l]), row(conv_ln_b[l]),
                            conv_w_pw[l].astype(BF16), rb=min(256, seq))
        x = _mix_ffn_call(x, mod3, y_att, y_ssm, y_conv, row(att_out_norm[l]), row(ssm_out_norm[l]),
                          row(conv_out_norm[l]), w_out[l].astype(BF16), row(ffn2_norm[l]),
                          ffn2_w_gu[l].astype(BF16), ffn2_w_down[l].astype(BF16), fin,
                          final=(l == depth - 1), tm=tm)
    return x
```
